```python
import jax, jax.numpy as jnp
from jax import lax
import numpy as np

D_MODEL = 2048
BATCH = 8
SEQ = 4096
DEPTH = 4

PLE_DIM = 256
CONV_WIDTH = 31
HEAD_DIM = D_MODEL // 16
CONV_HEADS = 8
POOL_GROUPS = 4
FOURIER_HEADS = 4
CONV_DIM = CONV_HEADS * HEAD_DIM
POOL_DIM = POOL_GROUPS * HEAD_DIM
FOURIER_DIM = FOURIER_HEADS * HEAD_DIM
MIX_DIM = CONV_DIM + POOL_DIM + FOURIER_DIM
IN_DIM = 2 * CONV_DIM + POOL_DIM + FOURIER_DIM
POOL_WINDOWS = (2, 4, 8, 16)
D_FF = -(-8 * D_MODEL // (3 * 256)) * 256
ALPHA = (2 * DEPTH) ** 0.25
BETA = (8 * DEPTH) ** -0.25
LN_EPS = 1e-5

kernel_name = "hybrid_conv_pool_fourier_deepnorm_encoder"


def layer_norm(x, g, b):
    xf = x.astype(jnp.float32)
    mu = jnp.mean(xf, axis=-1, keepdims=True)
    var = jnp.mean(jnp.square(xf - mu), axis=-1, keepdims=True)
    y = (xf - mu) * lax.rsqrt(var + LN_EPS) * g.astype(jnp.float32) + b.astype(jnp.float32)
    return y.astype(x.dtype)


def conformer_conv(val, gate, w_dw, b_dw, ln_g, ln_b):
    u = val * jax.nn.sigmoid(gate)
    y = lax.conv_general_dilated(
        u, w_dw[:, None, :].astype(u.dtype), window_strides=(1,),
        padding=[(CONV_WIDTH // 2, CONV_WIDTH // 2)],
        dimension_numbers=("NWC", "WIO", "NWC"),
        feature_group_count=CONV_DIM) + b_dw
    return jax.nn.silu(layer_norm(y, ln_g, ln_b))


def multiscale_pool(u, w_pool, b_pool, scale):
    B, S, _ = u.shape
    ug = u.reshape(B, S, POOL_GROUPS, HEAD_DIM).astype(jnp.float32)
    t = jnp.arange(S)
    outs = []
    for g, k in enumerate(POOL_WINDOWS):
        xg = ug[:, :, g, :]
        cs = jnp.concatenate([jnp.zeros((B, 1, HEAD_DIM), jnp.float32),
                              jnp.cumsum(xg, axis=1)], axis=1)
        lo = jnp.clip(t - k // 2, 0, S)
        hi = jnp.clip(t + k // 2, 0, S)
        cnt = (hi - lo).astype(jnp.float32)[None, :, None]
        mean = (jnp.take(cs, hi, axis=1) - jnp.take(cs, lo, axis=1)) / cnt
        outs.append(mean - xg)
    pooled = jnp.stack(outs, axis=2).astype(u.dtype)
    y = jnp.einsum("bsgc,gcd->bsgd", pooled, w_pool) + b_pool
    y = y * scale.reshape(POOL_GROUPS, HEAD_DIM)
    return y.reshape(B, S, POOL_DIM)


def fourier_mix(u, w_f, b_f):
    B, S, _ = u.shape
    uh = u.reshape(B, S, FOURIER_HEADS, HEAD_DIM).astype(jnp.float32)
    zf = jnp.fft.fftn(uh, axes=(1, 3), norm="ortho").real.astype(u.dtype)
    y = jnp.einsum("bshc,hcd->bshd", zf, w_f) + b_f
    return y.reshape(B, S, FOURIER_DIM)


def setup_inputs(seed: int = 0) -> dict:
    key = jax.random.key(seed)
    ks = jax.random.split(key, 32)
    f32 = jnp.float32

    def nrm(k, shape, scale):
        return jax.random.normal(k, shape, f32) * scale

    L = DEPTH
    return {
        "x": nrm(ks[0], (BATCH, SEQ, D_MODEL), 1.0),
        "p": nrm(ks[1], (DEPTH, BATCH, SEQ, PLE_DIM), 1.0),
        "w_in": nrm(ks[2], (L, D_MODEL, IN_DIM), D_MODEL ** -0.5),
        "b_in": nrm(ks[3], (L, IN_DIM), 0.02),
        "w_dw": nrm(ks[4], (L, CONV_WIDTH, CONV_DIM), CONV_WIDTH ** -0.5),
        "b_dw": nrm(ks[5], (L, CONV_DIM), 0.02),
        "conv_ln_g": 1.0 + nrm(ks[6], (L, CONV_DIM), 0.02),
        "conv_ln_b": nrm(ks[7], (L, CONV_DIM), 0.02),
        "w_pool": nrm(ks[8], (L, POOL_GROUPS, HEAD_DIM, HEAD_DIM), HEAD_DIM ** -0.5),
        "b_pool": nrm(ks[9], (L, POOL_GROUPS, HEAD_DIM), 0.02),
        "pool_scale": 1.0 + nrm(ks[10], (L, POOL_DIM), 0.02),
        "w_fourier": nrm(ks[11], (L, FOURIER_HEADS, HEAD_DIM, HEAD_DIM), HEAD_DIM ** -0.5),
        "b_fourier": nrm(ks[12], (L, FOURIER_HEADS, HEAD_DIM), 0.02),
        "w_out": nrm(ks[13], (L, MIX_DIM, D_MODEL), BETA * MIX_DIM ** -0.5),
        "b_out": nrm(ks[14], (L, D_MODEL), 0.02),
        "ln1_g": 1.0 + nrm(ks[15], (L, D_MODEL), 0.02),
        "ln1_b": nrm(ks[16], (L, D_MODEL), 0.02),
        "w_gate": nrm(ks[17], (L, D_MODEL, D_FF), D_MODEL ** -0.5),
        "w_up": nrm(ks[18], (L, D_MODEL, D_FF), D_MODEL ** -0.5),
        "w_down": nrm(ks[19], (L, D_FF, D_MODEL), BETA * D_FF ** -0.5),
        "w_ple": nrm(ks[20], (L, PLE_DIM, D_MODEL), BETA * PLE_DIM ** -0.5),
        "w_ple_gate": nrm(ks[21], (L, D_MODEL, D_MODEL), D_MODEL ** -0.5),
        "ln2_g": 1.0 + nrm(ks[22], (L, D_MODEL), 0.02),
        "ln2_b": nrm(ks[23], (L, D_MODEL), 0.02),
    }


def reference(x, p, w_in, b_in, w_dw, b_dw, conv_ln_g, conv_ln_b, w_pool, b_pool,
              pool_scale, w_fourier, b_fourier, w_out, b_out, ln1_g, ln1_b,
              w_gate, w_up, w_down, w_ple, w_ple_gate, ln2_g, ln2_b):
    s1 = CONV_DIM
    s2 = 2 * CONV_DIM
    s3 = 2 * CONV_DIM + POOL_DIM
    for i in range(DEPTH):
        z = jnp.einsum("bsd,de->bse", x, w_in[i]) + b_in[i]
        y_conv = conformer_conv(z[..., :s1], z[..., s1:s2], w_dw[i], b_dw[i],
                                conv_ln_g[i], conv_ln_b[i])
        y_pool = multiscale_pool(z[..., s2:s3], w_pool[i], b_pool[i], pool_scale[i])
        y_four = fourier_mix(z[..., s3:], w_fourier[i], b_fourier[i])
        mixed = jnp.concatenate([y_conv, y_pool, y_four], axis=-1)
        mix_out = jnp.einsum("bsm,md->bsd", mixed, w_out[i]) + b_out[i]
        x = layer_norm(ALPHA * x + mix_out, ln1_g[i], ln1_b[i])
        hid = jax.nn.silu(jnp.einsum("bsd,df->bsf", x, w_gate[i])) * \
            jnp.einsum("bsd,df->bsf", x, w_up[i])
        ffn = jnp.einsum("bsf,fd->bsd", hid, w_down[i])
        ple = jax.nn.sigmoid(jnp.einsum("bsd,de->bse", x, w_ple_gate[i])) * \
            jnp.einsum("bsq,qd->bsd", p[i], w_ple[i])
        x = layer_norm(ALPHA * x + ffn + ple, ln2_g[i], ln2_b[i])
    return x
```

```python
import functools
import math

import jax
import jax.numpy as jnp
from jax import lax
from jax.experimental import pallas as pl
from jax.experimental.pallas import tpu as pltpu

D_MODEL = 2048
PLE_DIM = 256
CONV_WIDTH = 31
HEAD_DIM = 128
CONV_DIM = 1024
POOL_GROUPS = 4
POOL_DIM = 512
FOURIER_HEADS = 4
FOURIER_DIM = 512
MIX_DIM = 2048
IN_DIM = 3072
POOL_WINDOWS = (2, 4, 8, 16)
D_FF = 5632
LN_EPS = 1e-5

CONV_HALO = 16
POOL_HALO = 8
V7X_VMEM_LIMIT = 56 * 1024 * 1024

BF16 = jnp.bfloat16
F32 = jnp.float32


def _dot(a, b):
    return jnp.dot(a, b, preferred_element_type=F32)


def _layer_norm_rows(v, g, b):
    mu = jnp.mean(v, axis=-1, keepdims=True)
    d = v - mu
    var = jnp.mean(d * d, axis=-1, keepdims=True)
    return d * lax.rsqrt(var + LN_EPS) * g + b


def _const_spec(shape):
    nd = len(shape)
    return pl.BlockSpec(shape, lambda *_: (0,) * nd)


IN_CHUNK = 512


def _in_proj_kernel(x_ref, w_ref, b_ref, cs_ref, u_ref, zp_ref, ab_ref):
    xb = x_ref[...].astype(BF16)

    def z_cols(lo):
        return _dot(xb, w_ref[0, :, lo:lo + IN_CHUNK]) + b_ref[0, :, lo:lo + IN_CHUNK]

    for c in range(CONV_DIM // IN_CHUNK):
        val = z_cols(c * IN_CHUNK)
        gate = z_cols(CONV_DIM + c * IN_CHUNK)
        u_ref[:, c * IN_CHUNK:(c + 1) * IN_CHUNK] = val * jax.nn.sigmoid(gate)
    zp_ref[...] = z_cols(2 * CONV_DIM)
    zf = z_cols(2 * CONV_DIM + POOL_DIM).astype(BF16)
    for h in range(FOURIER_HEADS):
        ab = _dot(zf[:, h * HEAD_DIM:(h + 1) * HEAD_DIM], cs_ref[...])
        ab_ref[:, h * HEAD_DIM:(h + 1) * HEAD_DIM] = ab[:, :HEAD_DIM].astype(BF16)
        ab_ref[:, FOURIER_DIM + h * HEAD_DIM:FOURIER_DIM + (h + 1) * HEAD_DIM] = (
            ab[:, HEAD_DIM:].astype(BF16))


def _in_proj(x, w_in, b_in, cs, layer, tm):
    m = x.shape[0]
    return pl.pallas_call(
        _in_proj_kernel,
        grid=(m // tm,),
        in_specs=[
            pl.BlockSpec((tm, D_MODEL), lambda i: (i, 0)),
            pl.BlockSpec((1, D_MODEL, IN_DIM), lambda i: (layer, 0, 0)),
            pl.BlockSpec((1, 1, IN_DIM), lambda i: (layer, 0, 0)),
            _const_spec((HEAD_DIM, 2 * HEAD_DIM)),
        ],
        out_specs=[
            pl.BlockSpec((tm, CONV_DIM), lambda i: (i, 0)),
            pl.BlockSpec((tm, POOL_DIM), lambda i: (i, 0)),
            pl.BlockSpec((tm, 2 * FOURIER_DIM), lambda i: (i, 0)),
        ],
        out_shape=[
            jax.ShapeDtypeStruct((m, CONV_DIM), F32),
            jax.ShapeDtypeStruct((m, POOL_DIM), F32),
            jax.ShapeDtypeStruct((m, 2 * FOURIER_DIM), BF16),
        ],
        compiler_params=pltpu.CompilerParams(
            dimension_semantics=("arbitrary",), vmem_limit_bytes=V7X_VMEM_LIMIT),
        name="in_proj",
    )(x, w_in, b_in, cs)


CONV_ROWS = 128


def _mixer_kernel(u_ref, up_ref, un_ref, zp_ref, zpp_ref, zpn_ref, ab_ref, fc_ref, fs_ref,
                  wdw_ref, bdw_ref, cg_ref, cb_ref, wpool_ref, bpool_ref, pscale_ref,
                  wf_ref, bf_ref, out_ref, uext_ref, pext_ref, conv_ref, *, ts, seq):
    s = pl.program_id(1)
    n_s = pl.num_programs(1)
    first = s == 0
    last = s == n_s - 1

    uext_ref[0:CONV_HALO, :] = jnp.where(first, 0.0, up_ref[0])
    uext_ref[CONV_HALO:CONV_HALO + ts, :] = u_ref[0]
    uext_ref[CONV_HALO + ts:, :] = jnp.where(last, 0.0, un_ref[0])
    pad = CONV_WIDTH // 2
    for r0 in range(0, ts, CONV_ROWS):
        for c0 in range(0, CONV_DIM, HEAD_DIM):
            acc = jnp.zeros((CONV_ROWS, HEAD_DIM), F32) + bdw_ref[0, :, c0:c0 + HEAD_DIM]
            for k in range(CONV_WIDTH):
                start = CONV_HALO + r0 + k - pad
                acc = acc + (uext_ref[start:start + CONV_ROWS, c0:c0 + HEAD_DIM]
                             * wdw_ref[0, k:k + 1, c0:c0 + HEAD_DIM])
            conv_ref[r0:r0 + CONV_ROWS, c0:c0 + HEAD_DIM] = acc
    y = _layer_norm_rows(conv_ref[...], cg_ref[0], cb_ref[0])
    out_ref[0, :, 0:CONV_DIM] = (y * jax.nn.sigmoid(y)).astype(BF16)

    pext_ref[0:POOL_HALO, :] = jnp.where(first, 0.0, zpp_ref[0])
    pext_ref[POOL_HALO:POOL_HALO + ts, :] = zp_ref[0]
    pext_ref[POOL_HALO + ts:, :] = jnp.where(last, 0.0, zpn_ref[0])
    t = s * ts + lax.broadcasted_iota(jnp.int32, (ts, 1), 0)
    for g, k in enumerate(POOL_WINDOWS):
        lanes = slice(g * HEAD_DIM, (g + 1) * HEAD_DIM)
        tot = jnp.zeros((ts, HEAD_DIM), F32)
        for o in range(-(k // 2), k // 2):
            tot = tot + pext_ref[POOL_HALO + o:POOL_HALO + o + ts, lanes]
        cnt = (jnp.minimum(t + k // 2, seq) - jnp.maximum(t - k // 2, 0)).astype(F32)
        pooled = tot / cnt - zp_ref[0, :, lanes]
        yp = _dot(pooled.astype(BF16), wpool_ref[0, g]) + bpool_ref[0, :, lanes]
        out_ref[0, :, CONV_DIM + g * HEAD_DIM:CONV_DIM + (g + 1) * HEAD_DIM] = (
            yp * pscale_ref[0, :, lanes]).astype(BF16)

    zf = (_dot(fc_ref[...], ab_ref[0, :, 0:FOURIER_DIM])
          + _dot(fs_ref[...], ab_ref[0, :, FOURIER_DIM:])).astype(BF16)
    for h in range(FOURIER_HEADS):
        lanes = slice(h * HEAD_DIM, (h + 1) * HEAD_DIM)
        yf = _dot(zf[:, lanes], wf_ref[0, h]) + bf_ref[0, :, lanes]
        off = CONV_DIM + POOL_DIM + h * HEAD_DIM
        out_ref[0, :, off:off + HEAD_DIM] = yf.astype(BF16)


def _mixer(u, zp, ab, fc, fs, w_dw, b_dw, cg, cb, w_pool, b_pool, pscale, w_f, b_f, layer, ts):
    bsz, seq, _ = u.shape
    n_s = seq // ts
    ch = ts // CONV_HALO
    ph = ts // POOL_HALO

    def lay3(shape):
        return pl.BlockSpec((1,) + shape, lambda b, s: (layer,) + (0,) * len(shape))

    kern = functools.partial(_mixer_kernel, ts=ts, seq=seq)
    return pl.pallas_call(
        kern,
        grid=(bsz, n_s),
        in_specs=[
            pl.BlockSpec((1, ts, CONV_DIM), lambda b, s: (b, s, 0)),
            pl.BlockSpec((1, CONV_HALO, CONV_DIM),
                         lambda b, s: (b, jnp.maximum(s * ch - 1, 0), 0)),
            pl.BlockSpec((1, CONV_HALO, CONV_DIM),
                         lambda b, s: (b, jnp.minimum((s + 1) * ch, seq // CONV_HALO - 1), 0)),
            pl.BlockSpec((1, ts, POOL_DIM), lambda b, s: (b, s, 0)),
            pl.BlockSpec((1, POOL_HALO, POOL_DIM),
                         lambda b, s: (b, jnp.maximum(s * ph - 1, 0), 0)),
            pl.BlockSpec((1, POOL_HALO, POOL_DIM),
                         lambda b, s: (b, jnp.minimum((s + 1) * ph, seq // POOL_HALO - 1), 0)),
            pl.BlockSpec((1, seq, 2 * FOURIER_DIM), lambda b, s: (b, 0, 0)),
            pl.BlockSpec((ts, seq), lambda b, s: (s, 0)),
            pl.BlockSpec((ts, seq), lambda b, s: (s, 0)),
            lay3((CONV_WIDTH, CONV_DIM)),
            lay3((1, CONV_DIM)),
            lay3((1, CONV_DIM)),
            lay3((1, CONV_DIM)),
            lay3((POOL_GROUPS, HEAD_DIM, HEAD_DIM)),
            lay3((1, POOL_DIM)),
            lay3((1, POOL_DIM)),
            lay3((FOURIER_HEADS, HEAD_DIM, HEAD_DIM)),
            lay3((1, FOURIER_DIM)),
        ],
        out_specs=pl.BlockSpec((1, ts, MIX_DIM), lambda b, s: (b, s, 0)),
        out_shape=jax.ShapeDtypeStruct((bsz, seq, MIX_DIM), BF16),
        scratch_shapes=[
            pltpu.VMEM((ts + 2 * CONV_HALO, CONV_DIM), F32),
            pltpu.VMEM((ts + 2 * POOL_HALO, POOL_DIM), F32),
            pltpu.VMEM((ts, CONV_DIM), F32),
        ],
        compiler_params=pltpu.CompilerParams(
            dimension_semantics=("arbitrary", "arbitrary"), vmem_limit_bytes=V7X_VMEM_LIMIT),
        name="mixer",
    )(u, u, u, zp, zp, zp, ab, fc, fs, w_dw, b_dw, cg, cb, w_pool, b_pool, pscale, w_f, b_f)


def _out_proj_kernel(mixed_ref, x_ref, w_ref, b_ref, g_ref, beta_ref, o_ref, *, alpha):
    y = alpha * x_ref[...] + _dot(mixed_ref[...], w_ref[0]) + b_ref[0]
    o_ref[...] = _layer_norm_rows(y, g_ref[0], beta_ref[0])


def _out_proj(mixed, x, w_out, b_out, g, beta, layer, tm, alpha):
    m = x.shape[0]

    def lay3(shape):
        return pl.BlockSpec((1,) + shape, lambda i: (layer,) + (0,) * len(shape))

    return pl.pallas_call(
        functools.partial(_out_proj_kernel, alpha=alpha),
        grid=(m // tm,),
        in_specs=[
            pl.BlockSpec((tm, MIX_DIM), lambda i: (i, 0)),
            pl.BlockSpec((tm, D_MODEL), lambda i: (i, 0)),
            lay3((MIX_DIM, D_MODEL)),
            lay3((1, D_MODEL)),
            lay3((1, D_MODEL)),
            lay3((1, D_MODEL)),
        ],
        out_specs=pl.BlockSpec((tm, D_MODEL), lambda i: (i, 0)),
        out_shape=jax.ShapeDtypeStruct((m, D_MODEL), F32),
        compiler_params=pltpu.CompilerParams(
            dimension_semantics=("arbitrary",), vmem_limit_bytes=V7X_VMEM_LIMIT),
        name="out_proj",
    )(mixed, x, w_out, b_out, g, beta)


FF_CHUNK = 512
N_FF = D_FF // FF_CHUNK
PLE_CHUNK = 512
N_PLE = D_MODEL // PLE_CHUNK


def _ffn_kernel(x_ref, p_ref, wg_ref, wu_ref, wd_ref, wpg_ref, wple_ref, g_ref, beta_ref,
                o_ref, xb_ref, acc_ref, *, alpha):
    j = pl.program_id(1)

    @pl.when(j == 0)
    def _():
        xb_ref[...] = x_ref[...].astype(BF16)
        acc_ref[...] = jnp.zeros_like(acc_ref)

    @pl.when(j < N_FF)
    def _():
        xb = xb_ref[...]
        gate = _dot(xb, wg_ref[0])
        up = _dot(xb, wu_ref[0])
        hid = (gate * jax.nn.sigmoid(gate) * up).astype(BF16)
        acc_ref[...] += _dot(hid, wd_ref[0])

    for c in range(N_PLE):
        @pl.when(j == N_FF + c)
        def _(c=c):
            cols = slice(c * PLE_CHUNK, (c + 1) * PLE_CHUNK)
            gate = jax.nn.sigmoid(_dot(xb_ref[...], wpg_ref[0]))
            emb = _dot(p_ref[0].astype(BF16), wple_ref[0])
            acc_ref[:, cols] += gate * emb

    @pl.when(j == N_FF + N_PLE - 1)
    def _():
        y = alpha * x_ref[...] + acc_ref[...]
        o_ref[...] = _layer_norm_rows(y, g_ref[0], beta_ref[0])


def _ffn(x, p, w_gate, w_up, w_down, w_pg, w_ple, g, beta, layer, tm, alpha):
    m = x.shape[0]
    last_ff = N_FF - 1

    def ff_col(i, j):
        return (layer, 0, jnp.minimum(j, last_ff))

    def ple_col(i, j):
        return (layer, 0, jnp.maximum(j - N_FF, 0))

    return pl.pallas_call(
        functools.partial(_ffn_kernel, alpha=alpha),
        grid=(m // tm, N_FF + N_PLE),
        in_specs=[
            pl.BlockSpec((tm, D_MODEL), lambda i, j: (i, 0)),
            pl.BlockSpec((1, tm, PLE_DIM), lambda i, j: (layer, i, 0)),
            pl.BlockSpec((1, D_MODEL, FF_CHUNK), ff_col),
            pl.BlockSpec((1, D_MODEL, FF_CHUNK), ff_col),
            pl.BlockSpec((1, FF_CHUNK, D_MODEL), lambda i, j: (layer, jnp.minimum(j, last_ff), 0)),
            pl.BlockSpec((1, D_MODEL, PLE_CHUNK), ple_col),
            pl.BlockSpec((1, PLE_DIM, PLE_CHUNK), ple_col),
            pl.BlockSpec((1, 1, D_MODEL), lambda i, j: (layer, 0, 0)),
            pl.BlockSpec((1, 1, D_MODEL), lambda i, j: (layer, 0, 0)),
        ],
        out_specs=pl.BlockSpec((tm, D_MODEL), lambda i, j: (i, 0)),
        out_shape=jax.ShapeDtypeStruct((m, D_MODEL), F32),
        scratch_shapes=[
            pltpu.VMEM((tm, D_MODEL), BF16),
            pltpu.VMEM((tm, D_MODEL), F32),
        ],
        compiler_params=pltpu.CompilerParams(
            dimension_semantics=("arbitrary", "arbitrary"), vmem_limit_bytes=V7X_VMEM_LIMIT),
        name="ffn",
    )(x, p, w_gate, w_up, w_down, w_pg, w_ple, g, beta)


def _dft_tables(seq):
    c = jnp.arange(HEAD_DIM, dtype=jnp.int32)
    ang = ((c[:, None] * c[None, :]) % HEAD_DIM).astype(F32) * (2.0 * math.pi / HEAD_DIM)
    cs = jnp.concatenate([jnp.cos(ang), jnp.sin(ang)], axis=1) * (HEAD_DIM ** -0.5)
    r = 64
    k = jnp.arange(seq, dtype=jnp.int32)
    j1 = jnp.arange(seq // r, dtype=jnp.int32)
    j0 = jnp.arange(r, dtype=jnp.int32)
    a1 = ((r * j1[:, None] * k[None, :]) % seq).astype(F32) * (2.0 * math.pi / seq)
    a0 = ((j0[:, None] * k[None, :]) % seq).astype(F32) * (2.0 * math.pi / seq)
    c1, s1 = jnp.cos(a1)[:, None, :], jnp.sin(a1)[:, None, :]
    c0, s0 = jnp.cos(a0)[None, :, :], jnp.sin(a0)[None, :, :]
    scale = seq ** -0.5
    fc = ((c1 * c0 - s1 * s0) * scale).reshape(seq, seq)
    fs = ((s1 * c0 + c1 * s0) * (-scale)).reshape(seq, seq)
    return cs.astype(BF16), fc.astype(BF16), fs.astype(BF16)


def kernel(x, p, w_in, b_in, w_dw, b_dw, conv_ln_g, conv_ln_b, w_pool, b_pool, pool_scale,
           w_fourier, b_fourier, w_out, b_out, ln1_g, ln1_b, w_gate, w_up, w_down, w_ple,
           w_ple_gate, ln2_g, ln2_b):
    bsz, seq, d = x.shape
    depth = w_in.shape[0]
    m = bsz * seq
    alpha = (2 * depth) ** 0.25
    tm = 512
    ts = 512

    cs, fc, fs = _dft_tables(seq)
    w_in_b = w_in.astype(BF16)
    w_pool_b = w_pool.astype(BF16)
    w_f_b = w_fourier.astype(BF16)
    w_out_b = w_out.astype(BF16)
    w_gate_b = w_gate.astype(BF16)
    w_up_b = w_up.astype(BF16)
    w_down_b = w_down.astype(BF16)
    w_ple_b = w_ple.astype(BF16)
    w_pg_b = w_ple_gate.astype(BF16)

    def row3(a):
        return a.reshape(a.shape[0], 1, -1)

    b_in3, b_dw3, cg3, cb3 = row3(b_in), row3(b_dw), row3(conv_ln_g), row3(conv_ln_b)
    b_pool3, pscale3, b_f3 = row3(b_pool), row3(pool_scale), row3(b_fourier)
    b_out3, g1, be1, g2, be2 = row3(b_out), row3(ln1_g), row3(ln1_b), row3(ln2_g), row3(ln2_b)
    p2 = p.reshape(depth, m, PLE_DIM)

    xf = x.reshape(m, d)
    for l in range(depth):
        u, zp, ab = _in_proj(xf, w_in_b, b_in3, cs, l, tm)
        mixed = _mixer(u.reshape(bsz, seq, CONV_DIM), zp.reshape(bsz, seq, POOL_DIM),
                       ab.reshape(bsz, seq, 2 * FOURIER_DIM), fc, fs, w_dw, b_dw3, cg3, cb3,
                       w_pool_b, b_pool3, pscale3, w_f_b, b_f3, l, ts)
        xf = _out_proj(mixed.reshape(m, MIX_DIM), xf, w_out_b, b_out3, g1, be1, l, tm, alpha)
        xf = _ffn(xf, p2, w_gate_b, w_up_b, w_down_b, w_pg_b, w_ple_b, g2, be2, l, tm, alpha)
    return xf.reshape(bsz, seq, d)
```

```python
import functools
import math

import jax
import jax.numpy as jnp
from jax import lax
from jax.experimental import pallas as pl
from jax.experimental.pallas import tpu as pltpu

D_MODEL = 2048
PLE_DIM = 256
CONV_WIDTH = 31
HEAD_DIM = 128
CONV_DIM = 1024
POOL_GROUPS = 4
POOL_DIM = 512
FOURIER_HEADS = 4
FOURIER_DIM = 512
MIX_DIM = 2048
IN_DIM = 3072
POOL_WINDOWS = (2, 4, 8, 16)
D_FF = 5632
LN_EPS = 1e-5

LANES = 128
SUBLANES = 8
CONV_HALO = 16
POOL_HALO = 8
V7X_VMEM_LIMIT = 56 * 1024 * 1024

BF16 = jnp.bfloat16
F32 = jnp.float32


def _dot(a, b):
    return jnp.dot(a, b, preferred_element_type=F32)


def _layer_norm_rows(v, g, b):
    mu = jnp.mean(v, axis=-1, keepdims=True)
    d = v - mu
    var = jnp.mean(d * d, axis=-1, keepdims=True)
    return d * lax.rsqrt(var + LN_EPS) * g + b


def _const_spec(shape):
    nd = len(shape)
    return pl.BlockSpec(shape, lambda *_: (0,) * nd)


IN_CHUNK = 512


def _in_proj_kernel(x_ref, w_ref, b_ref, cs_ref, u_ref, zp_ref, ab_ref):
    xb = x_ref[...].astype(BF16)

    def z_cols(lo):
        return _dot(xb, w_ref[0, :, lo:lo + IN_CHUNK]) + b_ref[0, :, lo:lo + IN_CHUNK]

    for c in range(CONV_DIM // IN_CHUNK):
        val = z_cols(c * IN_CHUNK)
        gate = z_cols(CONV_DIM + c * IN_CHUNK)
        u_ref[:, c * IN_CHUNK:(c + 1) * IN_CHUNK] = val * jax.nn.sigmoid(gate)
    zp_ref[...] = z_cols(2 * CONV_DIM)
    zf = z_cols(2 * CONV_DIM + POOL_DIM).astype(BF16)
    for h in range(FOURIER_HEADS):
        ab = _dot(zf[:, h * HEAD_DIM:(h + 1) * HEAD_DIM], cs_ref[...])
        ab_ref[0, :, h * HEAD_DIM:(h + 1) * HEAD_DIM] = ab[:, :HEAD_DIM].astype(BF16)
        ab_ref[1, :, h * HEAD_DIM:(h + 1) * HEAD_DIM] = ab[:, HEAD_DIM:].astype(BF16)


def _in_proj(x, w_in, b_in, cs, layer, tm):
    m = x.shape[0]
    return pl.pallas_call(
        _in_proj_kernel,
        grid=(m // tm,),
        in_specs=[
            pl.BlockSpec((tm, D_MODEL), lambda i: (i, 0)),
            pl.BlockSpec((1, D_MODEL, IN_DIM), lambda i: (layer, 0, 0)),
            pl.BlockSpec((1, 1, IN_DIM), lambda i: (layer, 0, 0)),
            _const_spec((HEAD_DIM, 2 * HEAD_DIM)),
        ],
        out_specs=[
            pl.BlockSpec((tm, CONV_DIM), lambda i: (i, 0)),
            pl.BlockSpec((tm, POOL_DIM), lambda i: (i, 0)),
            pl.BlockSpec((2, tm, FOURIER_DIM), lambda i: (0, i, 0)),
        ],
        out_shape=[
            jax.ShapeDtypeStruct((m, CONV_DIM), F32),
            jax.ShapeDtypeStruct((m, POOL_DIM), F32),
            jax.ShapeDtypeStruct((2, m, FOURIER_DIM), BF16),
        ],
        compiler_params=pltpu.CompilerParams(
            dimension_semantics=("arbitrary",), vmem_limit_bytes=V7X_VMEM_LIMIT),
        name="in_proj",
    )(x, w_in, b_in, cs)


MIX_ROWS = 256
STRIP = 34
STRIP_ROWS = STRIP * SUBLANES
CONV_SLABS = CONV_DIM // LANES
UEXT_ROWS = 304
PEXT_ROWS = 288
CONV_GROUPS = (9, 9, 8, 8)
CONV_TAP_SPLITS = ((0, 16), (16, CONV_WIDTH))
DFT_PIECES = CONV_SLABS // 2
assert sum(CONV_GROUPS) == STRIP
LN_ROWS = 64


def _strip_rows(i):
    return pl.ds(i, SUBLANES, stride=STRIP)


def _strided(ref, slab, i):
    return ref.at[slab][_strip_rows(i), :]


def _zero_after(v):
    bits = lax.bitcast_convert_type(v, jnp.uint32)
    half = jnp.uint32(16)
    zero = lax.shift_right_logical(lax.shift_right_logical(bits, half), half)
    return lax.bitcast_convert_type(zero, F32)


def _mix_out_kernel(u_ref, up_ref, un_ref, zp_ref, zpp_ref, zpn_ref, ab_ref, f_ref, x_ref,
                    wdw_ref, bdw_ref, cg_ref, cb_ref, wpool_ref, bpool_ref, pscale_ref,
                    wf_ref, bf_ref, wout_ref, bout_ref, g_ref, beta_ref, o_ref,
                    uext_ref, conv_ref, pext_ref, psum_ref, zf_ref, mixed_ref, *, seq, alpha):
    ts = MIX_ROWS
    s = pl.program_id(1)
    first = s == 0
    last = s == pl.num_programs(1) - 1

    uprev = jnp.where(first, 0.0, up_ref[0])
    unext = jnp.where(last, 0.0, un_ref[0])
    for l in range(CONV_SLABS):
        lanes = slice(l * LANES, (l + 1) * LANES)
        uext_ref[l, 0:CONV_HALO, :] = uprev[:, lanes]
        uext_ref[l, CONV_HALO:CONV_HALO + ts, :] = u_ref[0, :, lanes]
        uext_ref[l, CONV_HALO + ts:2 * CONV_HALO + ts, :] = unext[:, lanes]
        uext_ref[l, 2 * CONV_HALO + ts:, :] = jnp.zeros(
            (UEXT_ROWS - 2 * CONV_HALO - ts, LANES), F32)
    kp = seq // DFT_PIECES
    for l in range(CONV_SLABS):
        krows = slice((l % DFT_PIECES) * kp, (l % DFT_PIECES + 1) * kp)
        part = _dot(f_ref[l], ab_ref[l // DFT_PIECES, 0, krows, :])
        if l == 0:
            zf_ref[...] = part
        else:
            zf_ref[...] += part

        bias = (jnp.broadcast_to(bdw_ref[0, l], (SUBLANES, LANES))
                + _zero_after(part[0:SUBLANES, 0:LANES]))
        i0 = 0
        for n in CONV_GROUPS:
            accs = [bias] * n
            for k_lo, k_hi in CONV_TAP_SPLITS:
                taps = {k: jnp.broadcast_to(wdw_ref[0, l, k:k + 1, :], (SUBLANES, LANES))
                        for k in range(k_lo, k_hi)}
                for i in range(i0 + 1 + k_lo, i0 + n + k_hi):
                    v = _strided(uext_ref, l, i)
                    for idx in range(max(i0, i - k_hi), min(i0 + n, i - k_lo)):
                        accs[idx - i0] = accs[idx - i0] + v * taps[i - idx - 1]
            for j, acc in enumerate(accs):
                conv_ref.at[l][_strip_rows(i0 + j), :] = acc
            i0 += n

    for h in range(FOURIER_HEADS):
        lanes = slice(h * HEAD_DIM, (h + 1) * HEAD_DIM)
        yf = _dot(zf_ref[:, lanes].astype(BF16), wf_ref[0, h]) + bf_ref[0, :, lanes]
        off = CONV_DIM + POOL_DIM + h * HEAD_DIM
        mixed_ref[:, off:off + HEAD_DIM] = yf.astype(BF16)

    zprev = jnp.where(first, 0.0, zpp_ref[0])
    znext = jnp.where(last, 0.0, zpn_ref[0])
    t = s * ts + lax.broadcasted_iota(jnp.int32, (ts, 1), 0)
    for g, k in enumerate(POOL_WINDOWS):
        lanes = slice(g * HEAD_DIM, (g + 1) * HEAD_DIM)
        pext_ref[g, 0:POOL_HALO, :] = zprev[:, lanes]
        pext_ref[g, POOL_HALO:POOL_HALO + ts, :] = zp_ref[0, :, lanes]
        pext_ref[g, POOL_HALO + ts:2 * POOL_HALO + ts, :] = znext[:, lanes]
        pext_ref[g, 2 * POOL_HALO + ts:, :] = jnp.zeros(
            (PEXT_ROWS - 2 * POOL_HALO - ts, LANES), F32)
        memo = {}

        def window(width, c, g=g, memo=memo):
            key = (width, c)
            if key not in memo:
                if width == 1:
                    memo[key] = _strided(pext_ref, g, c)
                elif width == 2:
                    memo[key] = window(1, c - 1) + window(1, c)
                else:
                    q = width // 4
                    memo[key] = window(width // 2, c - q) + window(width // 2, c + q)
            return memo[key]

        for idx in range(STRIP):
            psum_ref.at[g][_strip_rows(idx), :] = window(k, POOL_HALO + idx)
        cnt = (jnp.minimum(t + k // 2, seq) - jnp.maximum(t - k // 2, 0)).astype(F32)
        pooled = psum_ref[g, 0:ts, :] / cnt - zp_ref[0, :, lanes]
        yp = _dot(pooled.astype(BF16), wpool_ref[0, g]) + bpool_ref[0, :, lanes]
        mixed_ref[:, CONV_DIM + g * HEAD_DIM:CONV_DIM + (g + 1) * HEAD_DIM] = (
            yp * pscale_ref[0, :, lanes]).astype(BF16)

    for r0 in range(0, ts, LN_ROWS):
        rows = slice(r0, r0 + LN_ROWS)
        parts = [conv_ref[l, rows, :] for l in range(CONV_SLABS)]
        tot = parts[0]
        for v in parts[1:]:
            tot = tot + v
        mu = jnp.sum(tot, axis=-1, keepdims=True) * (1.0 / CONV_DIM)
        devs = [v - mu for v in parts]
        sq = devs[0] * devs[0]
        for d in devs[1:]:
            sq = sq + d * d
        rstd = lax.rsqrt(jnp.sum(sq, axis=-1, keepdims=True) * (1.0 / CONV_DIM) + LN_EPS)
        for l, d in enumerate(devs):
            lanes = slice(l * LANES, (l + 1) * LANES)
            y = d * rstd * cg_ref[0, :, lanes] + cb_ref[0, :, lanes]
            mixed_ref[rows, lanes] = (y * jax.nn.sigmoid(y)).astype(BF16)

    k1, k2 = CONV_DIM, CONV_DIM + POOL_DIM
    proj = (_dot(mixed_ref[:, k2:], wout_ref[0, k2:, :])
            + _dot(mixed_ref[:, k1:k2], wout_ref[0, k1:k2, :])
            + _dot(mixed_ref[:, :k1], wout_ref[0, :k1, :]))
    y = alpha * x_ref[0] + proj + bout_ref[0]
    o_ref[0] = _layer_norm_rows(y, g_ref[0], beta_ref[0])


def _mix_out(u, zp, ab, f, x, w_dw, b_dw, cg, cb, w_pool, b_pool, pscale, w_f, b_f,
             w_out, b_out, g, beta, layer, alpha):
    bsz, seq, _ = u.shape
    ts = MIX_ROWS
    n_s = seq // ts
    ch = ts // CONV_HALO
    ph = ts // POOL_HALO

    def lay3(shape, **kw):
        return pl.BlockSpec((1,) + shape, lambda b, s: (layer,) + (0,) * len(shape), **kw)

    kern = functools.partial(_mix_out_kernel, seq=seq, alpha=alpha)
    return pl.pallas_call(
        kern,
        grid=(bsz, n_s),
        in_specs=[
            pl.BlockSpec((1, ts, CONV_DIM), lambda b, s: (b, s, 0)),
            pl.BlockSpec((1, CONV_HALO, CONV_DIM),
                         lambda b, s: (b, jnp.maximum(s * ch - 1, 0), 0)),
            pl.BlockSpec((1, CONV_HALO, CONV_DIM),
                         lambda b, s: (b, jnp.minimum((s + 1) * ch, seq // CONV_HALO - 1), 0)),
            pl.BlockSpec((1, ts, POOL_DIM), lambda b, s: (b, s, 0)),
            pl.BlockSpec((1, POOL_HALO, POOL_DIM),
                         lambda b, s: (b, jnp.maximum(s * ph - 1, 0), 0)),
            pl.BlockSpec((1, POOL_HALO, POOL_DIM),
                         lambda b, s: (b, jnp.minimum((s + 1) * ph, seq // POOL_HALO - 1), 0)),
            pl.BlockSpec((2, 1, seq, FOURIER_DIM), lambda b, s: (0, b, 0, 0),
                         pipeline_mode=pl.Buffered(1)),
            pl.BlockSpec((2 * DFT_PIECES, ts, seq // DFT_PIECES), lambda b, s: (0, s, 0)),
            pl.BlockSpec((1, ts, D_MODEL), lambda b, s: (b, s, 0)),
            lay3((CONV_SLABS, CONV_WIDTH, LANES)),
            lay3((CONV_SLABS, 1, LANES)),
            lay3((1, CONV_DIM)),
            lay3((1, CONV_DIM)),
            lay3((POOL_GROUPS, HEAD_DIM, HEAD_DIM)),
            lay3((1, POOL_DIM)),
            lay3((1, POOL_DIM)),
            lay3((FOURIER_HEADS, HEAD_DIM, HEAD_DIM)),
            lay3((1, FOURIER_DIM)),
            lay3((MIX_DIM, D_MODEL), pipeline_mode=pl.Buffered(1)),
            lay3((1, D_MODEL)),
            lay3((1, D_MODEL)),
            lay3((1, D_MODEL)),
        ],
        out_specs=pl.BlockSpec((1, ts, D_MODEL), lambda b, s: (b, s, 0)),
        out_shape=jax.ShapeDtypeStruct((bsz, seq, D_MODEL), F32),
        scratch_shapes=[
            pltpu.VMEM((CONV_SLABS, UEXT_ROWS, LANES), F32),
            pltpu.VMEM((CONV_SLABS, STRIP_ROWS, LANES), F32),
            pltpu.VMEM((POOL_GROUPS, PEXT_ROWS, LANES), F32),
            pltpu.VMEM((POOL_GROUPS, STRIP_ROWS, LANES), F32),
            pltpu.VMEM((ts, FOURIER_DIM), F32),
            pltpu.VMEM((ts, MIX_DIM), BF16),
        ],
        compiler_params=pltpu.CompilerParams(
            dimension_semantics=("arbitrary", "arbitrary"), vmem_limit_bytes=V7X_VMEM_LIMIT),
        name="mix_out",
    )(u, u, u, zp, zp, zp, ab, f, x, w_dw, b_dw, cg, cb, w_pool, b_pool, pscale, w_f, b_f,
      w_out, b_out, g, beta)


FF_CHUNK = 512
N_FF = D_FF // FF_CHUNK
FFN_TAIL_ROWS = 256


def _ffn_kernel(x_ref, p_ref, wg_ref, wu_ref, wd_ref, wpg_ref, wple_ref, g_ref, beta_ref,
                o_ref, xb_ref, acc_ref, *, alpha):
    j = pl.program_id(1)

    @pl.when(j == 0)
    def _():
        xb_ref[...] = x_ref[...].astype(BF16)
        acc_ref[...] = jnp.zeros_like(acc_ref)

    @pl.when(j < N_FF)
    def _():
        xb = xb_ref[...]
        gate = _dot(xb, wg_ref[0])
        up = _dot(xb, wu_ref[0])
        hid = (gate * jax.nn.sigmoid(gate) * up).astype(BF16)
        acc_ref[...] += _dot(hid, wd_ref[0])

    @pl.when(j == N_FF)
    def _():
        for r0 in range(0, x_ref.shape[0], FFN_TAIL_ROWS):
            rows = slice(r0, r0 + FFN_TAIL_ROWS)
            gate = jax.nn.sigmoid(_dot(xb_ref[rows, :], wpg_ref[0]))
            emb = _dot(p_ref[0, rows, :].astype(BF16), wple_ref[0])
            y = alpha * x_ref[rows, :] + acc_ref[rows, :] + gate * emb
            o_ref[rows, :] = _layer_norm_rows(y, g_ref[0], beta_ref[0])


def _ffn(x, p, w_gate, w_up, w_down, w_pg, w_ple, g, beta, layer, tm, alpha):
    m = x.shape[0]
    last_ff = N_FF - 1

    def ff_col(i, j):
        return (layer, 0, jnp.minimum(j, last_ff))

    def lay3(shape, **kw):
        return pl.BlockSpec((1,) + shape, lambda i, j: (layer,) + (0,) * len(shape), **kw)

    return pl.pallas_call(
        functools.partial(_ffn_kernel, alpha=alpha),
        grid=(m // tm, N_FF + 1),
        in_specs=[
            pl.BlockSpec((tm, D_MODEL), lambda i, j: (i, 0)),
            pl.BlockSpec((1, tm, PLE_DIM), lambda i, j: (layer, i, 0)),
            pl.BlockSpec((1, D_MODEL, FF_CHUNK), ff_col),
            pl.BlockSpec((1, D_MODEL, FF_CHUNK), ff_col),
            pl.BlockSpec((1, FF_CHUNK, D_MODEL), lambda i, j: (layer, jnp.minimum(j, last_ff), 0)),
            lay3((D_MODEL, D_MODEL), pipeline_mode=pl.Buffered(1)),
            lay3((PLE_DIM, D_MODEL), pipeline_mode=pl.Buffered(1)),
            lay3((1, D_MODEL)),
            lay3((1, D_MODEL)),
        ],
        out_specs=pl.BlockSpec((tm, D_MODEL), lambda i, j: (i, 0)),
        out_shape=jax.ShapeDtypeStruct((m, D_MODEL), F32),
        scratch_shapes=[
            pltpu.VMEM((tm, D_MODEL), BF16),
            pltpu.VMEM((tm, D_MODEL), F32),
        ],
        compiler_params=pltpu.CompilerParams(
            dimension_semantics=("arbitrary", "arbitrary"), vmem_limit_bytes=V7X_VMEM_LIMIT),
        name="ffn",
    )(x, p, w_gate, w_up, w_down, w_pg, w_ple, g, beta)


def _dft_tables(seq):
    c = jnp.arange(HEAD_DIM, dtype=jnp.int32)
    ang = ((c[:, None] * c[None, :]) % HEAD_DIM).astype(F32) * (2.0 * math.pi / HEAD_DIM)
    cs = jnp.concatenate([jnp.cos(ang), jnp.sin(ang)], axis=1) * (HEAD_DIM ** -0.5)
    r = 64
    k = jnp.arange(seq, dtype=jnp.int32)
    j1 = jnp.arange(seq // r, dtype=jnp.int32)
    j0 = jnp.arange(r, dtype=jnp.int32)
    a1 = ((r * j1[:, None] * k[None, :]) % seq).astype(F32) * (2.0 * math.pi / seq)
    a0 = ((j0[:, None] * k[None, :]) % seq).astype(F32) * (2.0 * math.pi / seq)
    c1, s1 = jnp.cos(a1)[:, None, :], jnp.sin(a1)[:, None, :]
    c0, s0 = jnp.cos(a0)[None, :, :], jnp.sin(a0)[None, :, :]
    scale = seq ** -0.5
    fc = ((c1 * c0 - s1 * s0) * scale).reshape(seq, seq)
    fs = ((s1 * c0 + c1 * s0) * (-scale)).reshape(seq, seq)

    def pieces(tbl):
        return tbl.reshape(seq, DFT_PIECES, seq // DFT_PIECES).transpose(1, 0, 2)

    f = jnp.concatenate([pieces(fc), pieces(fs)], axis=0)
    return cs.astype(BF16), f.astype(BF16)


def kernel(x, p, w_in, b_in, w_dw, b_dw, conv_ln_g, conv_ln_b, w_pool, b_pool, pool_scale,
           w_fourier, b_fourier, w_out, b_out, ln1_g, ln1_b, w_gate, w_up, w_down, w_ple,
           w_ple_gate, ln2_g, ln2_b):
    bsz, seq, d = x.shape
    depth = w_in.shape[0]
    m = bsz * seq
    alpha = (2 * depth) ** 0.25
    tm = 512
    assert seq % MIX_ROWS == 0 and m % tm == 0 and seq % tm == 0

    cs, f_tbl = _dft_tables(seq)
    w_in_b = w_in.astype(BF16)
    w_pool_b = w_pool.astype(BF16)
    w_f_b = w_fourier.astype(BF16)
    w_out_b = w_out.astype(BF16)
    w_gate_b = w_gate.astype(BF16)
    w_up_b = w_up.astype(BF16)
    w_down_b = w_down.astype(BF16)
    w_ple_b = w_ple.astype(BF16)
    w_pg_b = w_ple_gate.astype(BF16)

    def row3(a):
        return a.reshape(a.shape[0], 1, -1)

    b_in3, cg3, cb3 = row3(b_in), row3(conv_ln_g), row3(conv_ln_b)
    w_dw4 = w_dw.reshape(depth, CONV_WIDTH, CONV_SLABS, LANES).transpose(0, 2, 1, 3)
    b_dw4 = b_dw.reshape(depth, CONV_SLABS, 1, LANES)
    b_pool3, pscale3, b_f3 = row3(b_pool), row3(pool_scale), row3(b_fourier)
    b_out3, g1, be1, g2, be2 = row3(b_out), row3(ln1_g), row3(ln1_b), row3(ln2_g), row3(ln2_b)
    p2 = p.reshape(depth, m, PLE_DIM)

    xf = x.reshape(m, d)
    for l in range(depth):
        u, zp, ab = _in_proj(xf, w_in_b, b_in3, cs, l, tm)
        x1 = _mix_out(u.reshape(bsz, seq, CONV_DIM), zp.reshape(bsz, seq, POOL_DIM),
                      ab.reshape(2, bsz, seq, FOURIER_DIM), f_tbl, xf.reshape(bsz, seq, d),
                      w_dw4, b_dw4, cg3, cb3, w_pool_b, b_pool3, pscale3, w_f_b, b_f3,
                      w_out_b, b_out3, g1, be1, l, alpha)
        xf = _ffn(x1.reshape(m, d), p2, w_gate_b, w_up_b, w_down_b, w_pg_b, w_ple_b,
                  g2, be2, l, tm, alpha)
    return xf.reshape(bsz, seq, d)
```

```python
import functools
import math

import jax
import jax.numpy as jnp
from jax import lax
from jax.experimental import pallas as pl
from jax.experimental.pallas import tpu as pltpu

D_MODEL = 2048
PLE_DIM = 256
CONV_WIDTH = 31
HEAD_DIM = 128
CONV_DIM = 1024
POOL_GROUPS = 4
POOL_DIM = 512
FOURIER_HEADS = 4
FOURIER_DIM = 512
MIX_DIM = 2048
IN_DIM = 3072
POOL_WINDOWS = (2, 4, 8, 16)
D_FF = 5632
LN_EPS = 1e-5

LANES = 128
SUBLANES = 8
CONV_HALO = 16
POOL_HALO = 8
V7X_VMEM_LIMIT = 56 * 1024 * 1024

BF16 = jnp.bfloat16
F32 = jnp.float32


def _dot(a, b):
    return jnp.dot(a, b, preferred_element_type=F32)


def _layer_norm_rows(v, g, b):
    mu = jnp.mean(v, axis=-1, keepdims=True)
    d = v - mu
    var = jnp.mean(d * d, axis=-1, keepdims=True)
    return d * lax.rsqrt(var + LN_EPS) * g + b


def _const_spec(shape):
    nd = len(shape)
    return pl.BlockSpec(shape, lambda *_: (0,) * nd)


IN_CHUNK = 512


def _in_proj_kernel(x_ref, w_ref, b_ref, cs_ref, u_ref, zp_ref, ab_ref):
    xb = x_ref[...].astype(BF16)

    def z_cols(lo):
        return _dot(xb, w_ref[0, :, lo:lo + IN_CHUNK]) + b_ref[0, :, lo:lo + IN_CHUNK]

    for c in range(CONV_DIM // IN_CHUNK):
        val = z_cols(c * IN_CHUNK)
        gate = z_cols(CONV_DIM + c * IN_CHUNK)
        u_ref[:, c * IN_CHUNK:(c + 1) * IN_CHUNK] = val * jax.nn.sigmoid(gate)
    zp_ref[...] = z_cols(2 * CONV_DIM)
    zf = z_cols(2 * CONV_DIM + POOL_DIM).astype(BF16)
    for h in range(FOURIER_HEADS):
        ab = _dot(zf[:, h * HEAD_DIM:(h + 1) * HEAD_DIM], cs_ref[...])
        ab_ref[0, :, h * HEAD_DIM:(h + 1) * HEAD_DIM] = ab[:, :HEAD_DIM].astype(BF16)
        ab_ref[1, :, h * HEAD_DIM:(h + 1) * HEAD_DIM] = ab[:, HEAD_DIM:].astype(BF16)


def _in_proj(x, w_in, b_in, cs, layer, tm):
    m = x.shape[0]
    return pl.pallas_call(
        _in_proj_kernel,
        grid=(m // tm,),
        in_specs=[
            pl.BlockSpec((tm, D_MODEL), lambda i: (i, 0)),
            pl.BlockSpec((1, D_MODEL, IN_DIM), lambda i: (layer, 0, 0)),
            pl.BlockSpec((1, 1, IN_DIM), lambda i: (layer, 0, 0)),
            _const_spec((HEAD_DIM, 2 * HEAD_DIM)),
        ],
        out_specs=[
            pl.BlockSpec((tm, CONV_DIM), lambda i: (i, 0)),
            pl.BlockSpec((tm, POOL_DIM), lambda i: (i, 0)),
            pl.BlockSpec((2, tm, FOURIER_DIM), lambda i: (0, i, 0)),
        ],
        out_shape=[
            jax.ShapeDtypeStruct((m, CONV_DIM), F32),
            jax.ShapeDtypeStruct((m, POOL_DIM), F32),
            jax.ShapeDtypeStruct((2, m, FOURIER_DIM), BF16),
        ],
        compiler_params=pltpu.CompilerParams(
            dimension_semantics=("arbitrary",), vmem_limit_bytes=V7X_VMEM_LIMIT),
        name="in_proj",
    )(x, w_in, b_in, cs)


MIX_ROWS = 256
STRIP = 34
STRIP_ROWS = STRIP * SUBLANES
CONV_SLABS = CONV_DIM // LANES
UEXT_ROWS = 304
PEXT_ROWS = 288
CONV_GROUPS = (9, 9, 8, 8)
CONV_TAP_SPLITS = ((0, 16), (16, CONV_WIDTH))
DFT_PIECES = CONV_SLABS // 2
assert sum(CONV_GROUPS) == STRIP
LN_ROWS = 64


def _strip_rows(i):
    return pl.ds(i, SUBLANES, stride=STRIP)


def _strided(ref, slab, i):
    return ref.at[slab][_strip_rows(i), :]


def _zero_after(v):
    bits = lax.bitcast_convert_type(v, jnp.uint32)
    half = jnp.uint32(16)
    zero = lax.shift_right_logical(lax.shift_right_logical(bits, half), half)
    return lax.bitcast_convert_type(zero, F32)


def _mix_out_kernel(u_ref, up_ref, un_ref, zp_ref, zpp_ref, zpn_ref, ab_ref, f_ref, x_ref,
                    wdw_ref, bdw_ref, cg_ref, cb_ref, wpool_ref, bpool_ref, pscale_ref,
                    wf_ref, bf_ref, wout_ref, bout_ref, g_ref, beta_ref, o_ref, ob_ref,
                    uext_ref, conv_ref, pext_ref, psum_ref, zf_ref, mixed_ref, *, seq, alpha):
    ts = MIX_ROWS
    s = pl.program_id(1)
    first = s == 0
    last = s == pl.num_programs(1) - 1

    uprev = jnp.where(first, 0.0, up_ref[0])
    unext = jnp.where(last, 0.0, un_ref[0])
    for l in range(CONV_SLABS):
        lanes = slice(l * LANES, (l + 1) * LANES)
        uext_ref[l, 0:CONV_HALO, :] = uprev[:, lanes]
        uext_ref[l, CONV_HALO:CONV_HALO + ts, :] = u_ref[0, :, lanes]
        uext_ref[l, CONV_HALO + ts:2 * CONV_HALO + ts, :] = unext[:, lanes]
        uext_ref[l, 2 * CONV_HALO + ts:, :] = jnp.zeros(
            (UEXT_ROWS - 2 * CONV_HALO - ts, LANES), F32)
    kp = seq // DFT_PIECES
    for l in range(CONV_SLABS):
        krows = slice((l % DFT_PIECES) * kp, (l % DFT_PIECES + 1) * kp)
        part = _dot(f_ref[l], ab_ref[l // DFT_PIECES, 0, krows, :])
        if l == 0:
            zf_ref[...] = part
        else:
            zf_ref[...] += part

        bias = (jnp.broadcast_to(bdw_ref[0, l], (SUBLANES, LANES))
                + _zero_after(part[0:SUBLANES, 0:LANES]))
        i0 = 0
        for n in CONV_GROUPS:
            accs = [bias] * n
            for k_lo, k_hi in CONV_TAP_SPLITS:
                taps = {k: jnp.broadcast_to(wdw_ref[0, l, k:k + 1, :], (SUBLANES, LANES))
                        for k in range(k_lo, k_hi)}
                for i in range(i0 + 1 + k_lo, i0 + n + k_hi):
                    v = _strided(uext_ref, l, i)
                    for idx in range(max(i0, i - k_hi), min(i0 + n, i - k_lo)):
                        accs[idx - i0] = accs[idx - i0] + v * taps[i - idx - 1]
            for j, acc in enumerate(accs):
                conv_ref.at[l][_strip_rows(i0 + j), :] = acc
            i0 += n

    for h in range(FOURIER_HEADS):
        lanes = slice(h * HEAD_DIM, (h + 1) * HEAD_DIM)
        yf = _dot(zf_ref[:, lanes].astype(BF16), wf_ref[0, h]) + bf_ref[0, :, lanes]
        off = CONV_DIM + POOL_DIM + h * HEAD_DIM
        mixed_ref[:, off:off + HEAD_DIM] = yf.astype(BF16)

    zprev = jnp.where(first, 0.0, zpp_ref[0])
    znext = jnp.where(last, 0.0, zpn_ref[0])
    t = s * ts + lax.broadcasted_iota(jnp.int32, (ts, 1), 0)
    for g, k in enumerate(POOL_WINDOWS):
        lanes = slice(g * HEAD_DIM, (g + 1) * HEAD_DIM)
        pext_ref[g, 0:POOL_HALO, :] = zprev[:, lanes]
        pext_ref[g, POOL_HALO:POOL_HALO + ts, :] = zp_ref[0, :, lanes]
        pext_ref[g, POOL_HALO + ts:2 * POOL_HALO + ts, :] = znext[:, lanes]
        pext_ref[g, 2 * POOL_HALO + ts:, :] = jnp.zeros(
            (PEXT_ROWS - 2 * POOL_HALO - ts, LANES), F32)
        memo = {}

        def window(width, c, g=g, memo=memo):
            key = (width, c)
            if key not in memo:
                if width == 1:
                    memo[key] = _strided(pext_ref, g, c)
                elif width == 2:
                    memo[key] = window(1, c - 1) + window(1, c)
                else:
                    q = width // 4
                    memo[key] = window(width // 2, c - q) + window(width // 2, c + q)
            return memo[key]

        for idx in range(STRIP):
            psum_ref.at[g][_strip_rows(idx), :] = window(k, POOL_HALO + idx)
        cnt = (jnp.minimum(t + k // 2, seq) - jnp.maximum(t - k // 2, 0)).astype(F32)
        pooled = psum_ref[g, 0:ts, :] / cnt - zp_ref[0, :, lanes]
        yp = _dot(pooled.astype(BF16), wpool_ref[0, g]) + bpool_ref[0, :, lanes]
        mixed_ref[:, CONV_DIM + g * HEAD_DIM:CONV_DIM + (g + 1) * HEAD_DIM] = (
            yp * pscale_ref[0, :, lanes]).astype(BF16)

    for r0 in range(0, ts, LN_ROWS):
        rows = slice(r0, r0 + LN_ROWS)
        parts = [conv_ref[l, rows, :] for l in range(CONV_SLABS)]
        tot = parts[0]
        for v in parts[1:]:
            tot = tot + v
        mu = jnp.sum(tot, axis=-1, keepdims=True) * (1.0 / CONV_DIM)
        devs = [v - mu for v in parts]
        sq = devs[0] * devs[0]
        for d in devs[1:]:
            sq = sq + d * d
        rstd = lax.rsqrt(jnp.sum(sq, axis=-1, keepdims=True) * (1.0 / CONV_DIM) + LN_EPS)
        for l, d in enumerate(devs):
            lanes = slice(l * LANES, (l + 1) * LANES)
            y = d * rstd * cg_ref[0, :, lanes] + cb_ref[0, :, lanes]
            mixed_ref[rows, lanes] = (y * jax.nn.sigmoid(y)).astype(BF16)

    k1, k2 = CONV_DIM, CONV_DIM + POOL_DIM
    proj = (_dot(mixed_ref[:, k2:], wout_ref[0, k2:, :])
            + _dot(mixed_ref[:, k1:k2], wout_ref[0, k1:k2, :])
            + _dot(mixed_ref[:, :k1], wout_ref[0, :k1, :]))
    y = alpha * x_ref[0] + proj + bout_ref[0]
    x1 = _layer_norm_rows(y, g_ref[0], beta_ref[0])
    o_ref[0] = x1
    ob_ref[0] = x1.astype(BF16)


def _mix_out(u, zp, ab, f, x, w_dw, b_dw, cg, cb, w_pool, b_pool, pscale, w_f, b_f,
             w_out, b_out, g, beta, layer, alpha):
    bsz, seq, _ = u.shape
    ts = MIX_ROWS
    n_s = seq // ts
    ch = ts // CONV_HALO
    ph = ts // POOL_HALO

    def lay3(shape, **kw):
        return pl.BlockSpec((1,) + shape, lambda b, s: (layer,) + (0,) * len(shape), **kw)

    kern = functools.partial(_mix_out_kernel, seq=seq, alpha=alpha)
    return pl.pallas_call(
        kern,
        grid=(bsz, n_s),
        in_specs=[
            pl.BlockSpec((1, ts, CONV_DIM), lambda b, s: (b, s, 0)),
            pl.BlockSpec((1, CONV_HALO, CONV_DIM),
                         lambda b, s: (b, jnp.maximum(s * ch - 1, 0), 0)),
            pl.BlockSpec((1, CONV_HALO, CONV_DIM),
                         lambda b, s: (b, jnp.minimum((s + 1) * ch, seq // CONV_HALO - 1), 0)),
            pl.BlockSpec((1, ts, POOL_DIM), lambda b, s: (b, s, 0)),
            pl.BlockSpec((1, POOL_HALO, POOL_DIM),
                         lambda b, s: (b, jnp.maximum(s * ph - 1, 0), 0)),
            pl.BlockSpec((1, POOL_HALO, POOL_DIM),
                         lambda b, s: (b, jnp.minimum((s + 1) * ph, seq // POOL_HALO - 1), 0)),
            pl.BlockSpec((2, 1, seq, FOURIER_DIM), lambda b, s: (0, b, 0, 0),
                         pipeline_mode=pl.Buffered(1)),
            pl.BlockSpec((2 * DFT_PIECES, ts, seq // DFT_PIECES), lambda b, s: (0, s, 0)),
            pl.BlockSpec((1, ts, D_MODEL), lambda b, s: (b, s, 0)),
            lay3((CONV_SLABS, CONV_WIDTH, LANES)),
            lay3((CONV_SLABS, 1, LANES)),
            lay3((1, CONV_DIM)),
            lay3((1, CONV_DIM)),
            lay3((POOL_GROUPS, HEAD_DIM, HEAD_DIM)),
            lay3((1, POOL_DIM)),
            lay3((1, POOL_DIM)),
            lay3((FOURIER_HEADS, HEAD_DIM, HEAD_DIM)),
            lay3((1, FOURIER_DIM)),
            lay3((MIX_DIM, D_MODEL), pipeline_mode=pl.Buffered(1)),
            lay3((1, D_MODEL)),
            lay3((1, D_MODEL)),
            lay3((1, D_MODEL)),
        ],
        out_specs=[pl.BlockSpec((1, ts, D_MODEL), lambda b, s: (b, s, 0)),
                   pl.BlockSpec((1, ts, D_MODEL), lambda b, s: (b, s, 0))],
        out_shape=[jax.ShapeDtypeStruct((bsz, seq, D_MODEL), F32),
                   jax.ShapeDtypeStruct((bsz, seq, D_MODEL), BF16)],
        scratch_shapes=[
            pltpu.VMEM((CONV_SLABS, UEXT_ROWS, LANES), F32),
            pltpu.VMEM((CONV_SLABS, STRIP_ROWS, LANES), F32),
            pltpu.VMEM((POOL_GROUPS, PEXT_ROWS, LANES), F32),
            pltpu.VMEM((POOL_GROUPS, STRIP_ROWS, LANES), F32),
            pltpu.VMEM((ts, FOURIER_DIM), F32),
            pltpu.VMEM((ts, MIX_DIM), BF16),
        ],
        compiler_params=pltpu.CompilerParams(
            dimension_semantics=("arbitrary", "arbitrary"), vmem_limit_bytes=V7X_VMEM_LIMIT),
        name="mix_out",
    )(u, u, u, zp, zp, zp, ab, f, x, w_dw, b_dw, cg, cb, w_pool, b_pool, pscale, w_f, b_f,
      w_out, b_out, g, beta)


FF_CHUNK = 512
N_FF = D_FF // FF_CHUNK
FFN_ROWS = 1024
FFN_TAIL_ROWS = 256
N_TAIL = FFN_ROWS // FFN_TAIL_ROWS


def _ffn_kernel(xb_ref, x_ref, p_ref, wg_ref, wu_ref, wd_ref, wpg_ref, wple_ref, g_ref, beta_ref,
                o_ref, acc_ref, *, alpha):
    j = pl.program_id(1)

    def down_proj():
        xb = xb_ref[...]
        gate = _dot(xb, wg_ref[0, 0])
        up = _dot(xb, wu_ref[0, 0])
        hid = (gate * jax.nn.sigmoid(gate) * up).astype(BF16)
        return _dot(hid, wd_ref[0])

    @pl.when(j == 0)
    def _():
        acc_ref[...] = down_proj()

    @pl.when(jnp.logical_and(j > 0, j < N_FF))
    def _():
        acc_ref[...] += down_proj()

    @pl.when(j >= N_FF)
    def _():
        rows = pl.ds(pl.multiple_of((j - N_FF) * FFN_TAIL_ROWS, FFN_TAIL_ROWS), FFN_TAIL_ROWS)
        gate = jax.nn.sigmoid(_dot(xb_ref[rows, :], wpg_ref[0]))
        emb = _dot(p_ref[0].astype(BF16), wple_ref[0])
        y = alpha * x_ref[...] + acc_ref[rows, :] + gate * emb
        o_ref[...] = _layer_norm_rows(y, g_ref[0], beta_ref[0])


def _ffn(xb, x, p, w_gate, w_up, w_down, w_pg, w_ple, g, beta, layer, alpha):
    m = x.shape[0]
    last_ff = N_FF - 1

    def ff_chunk(i, j):
        return (layer, jnp.minimum(j, last_ff), 0, 0)

    def tail_rows(i, j):
        return i * N_TAIL + jnp.maximum(j - N_FF, 0)

    def lay3(shape, **kw):
        return pl.BlockSpec((1,) + shape, lambda i, j: (layer,) + (0,) * len(shape), **kw)

    return pl.pallas_call(
        functools.partial(_ffn_kernel, alpha=alpha),
        grid=(m // FFN_ROWS, N_FF + N_TAIL),
        in_specs=[
            pl.BlockSpec((FFN_ROWS, D_MODEL), lambda i, j: (i, 0)),
            pl.BlockSpec((FFN_TAIL_ROWS, D_MODEL), lambda i, j: (tail_rows(i, j), 0)),
            pl.BlockSpec((1, FFN_TAIL_ROWS, PLE_DIM), lambda i, j: (layer, tail_rows(i, j), 0)),
            pl.BlockSpec((1, 1, D_MODEL, FF_CHUNK), ff_chunk),
            pl.BlockSpec((1, 1, D_MODEL, FF_CHUNK), ff_chunk),
            pl.BlockSpec((1, FF_CHUNK, D_MODEL), lambda i, j: (layer, jnp.minimum(j, last_ff), 0)),
            lay3((D_MODEL, D_MODEL), pipeline_mode=pl.Buffered(1)),
            lay3((PLE_DIM, D_MODEL), pipeline_mode=pl.Buffered(1)),
            lay3((1, D_MODEL)),
            lay3((1, D_MODEL)),
        ],
        out_specs=pl.BlockSpec((FFN_TAIL_ROWS, D_MODEL), lambda i, j: (tail_rows(i, j), 0)),
        out_shape=jax.ShapeDtypeStruct((m, D_MODEL), F32),
        scratch_shapes=[pltpu.VMEM((FFN_ROWS, D_MODEL), F32)],
        compiler_params=pltpu.CompilerParams(
            dimension_semantics=("arbitrary", "arbitrary"), vmem_limit_bytes=V7X_VMEM_LIMIT),
        name="ffn",
    )(xb, x, p, w_gate, w_up, w_down, w_pg, w_ple, g, beta)


def _dft_tables(seq):
    c = jnp.arange(HEAD_DIM, dtype=jnp.int32)
    ang = ((c[:, None] * c[None, :]) % HEAD_DIM).astype(F32) * (2.0 * math.pi / HEAD_DIM)
    cs = jnp.concatenate([jnp.cos(ang), jnp.sin(ang)], axis=1) * (HEAD_DIM ** -0.5)
    r = 64
    k = jnp.arange(seq, dtype=jnp.int32)
    j1 = jnp.arange(seq // r, dtype=jnp.int32)
    j0 = jnp.arange(r, dtype=jnp.int32)
    a1 = ((r * j1[:, None] * k[None, :]) % seq).astype(F32) * (2.0 * math.pi / seq)
    a0 = ((j0[:, None] * k[None, :]) % seq).astype(F32) * (2.0 * math.pi / seq)
    kp = seq // DFT_PIECES

    def pieces(small):
        return small.reshape(small.shape[0], DFT_PIECES, kp).transpose(1, 0, 2)

    c1, s1 = pieces(jnp.cos(a1))[:, :, None, :], pieces(jnp.sin(a1))[:, :, None, :]
    c0, s0 = pieces(jnp.cos(a0))[:, None, :, :], pieces(jnp.sin(a0))[:, None, :, :]
    scale = seq ** -0.5
    fc = ((c1 * c0 - s1 * s0) * scale).reshape(DFT_PIECES, seq, kp)
    fs = ((s1 * c0 + c1 * s0) * (-scale)).reshape(DFT_PIECES, seq, kp)
    f = jnp.concatenate([fc.astype(BF16), fs.astype(BF16)], axis=0)
    return cs.astype(BF16), f


def kernel(x, p, w_in, b_in, w_dw, b_dw, conv_ln_g, conv_ln_b, w_pool, b_pool, pool_scale,
           w_fourier, b_fourier, w_out, b_out, ln1_g, ln1_b, w_gate, w_up, w_down, w_ple,
           w_ple_gate, ln2_g, ln2_b):
    bsz, seq, d = x.shape
    depth = w_in.shape[0]
    m = bsz * seq
    alpha = (2 * depth) ** 0.25
    tm = 512
    assert seq % MIX_ROWS == 0 and seq % tm == 0 and m % FFN_ROWS == 0

    cs, f_tbl = _dft_tables(seq)
    w_in_b = w_in.astype(BF16)
    w_pool_b = w_pool.astype(BF16)
    w_f_b = w_fourier.astype(BF16)
    w_out_b = w_out.astype(BF16)
    def ff_chunks(w):
        return w.astype(BF16).reshape(depth, d, N_FF, FF_CHUNK).transpose(0, 2, 1, 3)

    w_gate_b = ff_chunks(w_gate)
    w_up_b = ff_chunks(w_up)
    w_down_b = w_down.astype(BF16)
    w_ple_b = w_ple.astype(BF16)
    w_pg_b = w_ple_gate.astype(BF16)

    def row3(a):
        return a.reshape(a.shape[0], 1, -1)

    b_in3, cg3, cb3 = row3(b_in), row3(conv_ln_g), row3(conv_ln_b)
    w_dw4 = w_dw.reshape(depth, CONV_WIDTH, CONV_SLABS, LANES).transpose(0, 2, 1, 3)
    b_dw4 = b_dw.reshape(depth, CONV_SLABS, 1, LANES)
    b_pool3, pscale3, b_f3 = row3(b_pool), row3(pool_scale), row3(b_fourier)
    b_out3, g1, be1, g2, be2 = row3(b_out), row3(ln1_g), row3(ln1_b), row3(ln2_g), row3(ln2_b)
    p2 = p.reshape(depth, m, PLE_DIM)

    xf = x.reshape(m, d)
    for l in range(depth):
        u, zp, ab = _in_proj(xf, w_in_b, b_in3, cs, l, tm)
        x1, x1b = _mix_out(u.reshape(bsz, seq, CONV_DIM), zp.reshape(bsz, seq, POOL_DIM),
                           ab.reshape(2, bsz, seq, FOURIER_DIM), f_tbl, xf.reshape(bsz, seq, d),
                           w_dw4, b_dw4, cg3, cb3, w_pool_b, b_pool3, pscale3, w_f_b, b_f3,
                           w_out_b, b_out3, g1, be1, l, alpha)
        xf = _ffn(x1b.reshape(m, d), x1.reshape(m, d), p2, w_gate_b, w_up_b, w_down_b, w_pg_b,
                  w_ple_b, g2, be2, l, alpha)
    return xf.reshape(bsz, seq, d)
```

```python
import functools
import math

import jax
import jax.numpy as jnp
from jax import lax
from jax.experimental import pallas as pl
from jax.experimental.pallas import tpu as pltpu

D_MODEL = 2048
PLE_DIM = 256
CONV_WIDTH = 31
HEAD_DIM = 128
CONV_DIM = 1024
POOL_GROUPS = 4
POOL_DIM = 512
FOURIER_HEADS = 4
FOURIER_DIM = 512
MIX_DIM = 2048
IN_DIM = 3072
POOL_WINDOWS = (2, 4, 8, 16)
D_FF = 5632
LN_EPS = 1e-5

LANES = 128
SUBLANES = 8
CONV_HALO = 16
POOL_HALO = 8
V7X_VMEM_LIMIT = 56 * 1024 * 1024

BF16 = jnp.bfloat16
F32 = jnp.float32


def _dot(a, b):
    return jnp.dot(a, b, preferred_element_type=F32)


def _layer_norm_rows(v, g, b):
    mu = jnp.mean(v, axis=-1, keepdims=True)
    d = v - mu
    var = jnp.mean(d * d, axis=-1, keepdims=True)
    return d * lax.rsqrt(var + LN_EPS) * g + b


def _const_spec(shape):
    nd = len(shape)
    return pl.BlockSpec(shape, lambda *_: (0,) * nd)


IN_CHUNK = 512
IN_ROWS = 1024


def _in_proj_kernel(x_ref, w_ref, b_ref, cs_ref, u_ref, zp_ref, ab_ref):
    xb = x_ref[...].astype(BF16)

    def z_cols(lo):
        return _dot(xb, w_ref[0, :, lo:lo + IN_CHUNK]) + b_ref[0, :, lo:lo + IN_CHUNK]

    for c in range(CONV_DIM // IN_CHUNK):
        val = z_cols(c * IN_CHUNK)
        gate = z_cols(CONV_DIM + c * IN_CHUNK)
        u_ref[:, c * IN_CHUNK:(c + 1) * IN_CHUNK] = val * jax.nn.sigmoid(gate)
    zp_ref[...] = z_cols(2 * CONV_DIM)
    zf = z_cols(2 * CONV_DIM + POOL_DIM).astype(BF16)
    for h in range(FOURIER_HEADS):
        ab = _dot(zf[:, h * HEAD_DIM:(h + 1) * HEAD_DIM], cs_ref[...])
        ab_ref[0, :, h * HEAD_DIM:(h + 1) * HEAD_DIM] = ab[:, :HEAD_DIM].astype(BF16)
        ab_ref[1, :, h * HEAD_DIM:(h + 1) * HEAD_DIM] = ab[:, HEAD_DIM:].astype(BF16)


def _in_proj(x, w_in, b_in, cs, layer, tm):
    m = x.shape[0]
    return pl.pallas_call(
        _in_proj_kernel,
        grid=(m // tm,),
        in_specs=[
            pl.BlockSpec((tm, D_MODEL), lambda i: (i, 0)),
            pl.BlockSpec((1, D_MODEL, IN_DIM), lambda i: (layer, 0, 0),
                         pipeline_mode=pl.Buffered(1)),
            pl.BlockSpec((1, 1, IN_DIM), lambda i: (layer, 0, 0)),
            _const_spec((HEAD_DIM, 2 * HEAD_DIM)),
        ],
        out_specs=[
            pl.BlockSpec((tm, CONV_DIM), lambda i: (i, 0)),
            pl.BlockSpec((tm, POOL_DIM), lambda i: (i, 0)),
            pl.BlockSpec((2, tm, FOURIER_DIM), lambda i: (0, i, 0)),
        ],
        out_shape=[
            jax.ShapeDtypeStruct((m, CONV_DIM), F32),
            jax.ShapeDtypeStruct((m, POOL_DIM), F32),
            jax.ShapeDtypeStruct((2, m, FOURIER_DIM), BF16),
        ],
        compiler_params=pltpu.CompilerParams(
            dimension_semantics=("arbitrary",), vmem_limit_bytes=V7X_VMEM_LIMIT),
        name="in_proj",
    )(x, w_in, b_in, cs)


MIX_ROWS = 256
STRIP = 34
STRIP_ROWS = STRIP * SUBLANES
CONV_SLABS = CONV_DIM // LANES
UEXT_ROWS = 304
PEXT_ROWS = 288
CONV_GROUPS = (9, 9, 8, 8)
CONV_CHAINED_SLABS = 2
CONV_GROUPS_CHAINED = (3,) * 11 + (1,)
CONV_TAP_SPLITS = ((0, 16), (16, CONV_WIDTH))
DFT_PIECES = CONV_SLABS // 2
assert sum(CONV_GROUPS) == STRIP and sum(CONV_GROUPS_CHAINED) == STRIP
LN_ROWS = 64


def _strip_rows(i):
    return pl.ds(i, SUBLANES, stride=STRIP)


def _strided(ref, slab, i):
    return ref.at[slab][_strip_rows(i), :]


def _zero_after(v):
    bits = lax.bitcast_convert_type(v, jnp.uint32)
    half = jnp.uint32(16)
    zero = lax.shift_right_logical(lax.shift_right_logical(bits, half), half)
    return lax.bitcast_convert_type(zero, F32)


def _mix_out_kernel(u_ref, up_ref, un_ref, zp_ref, zpp_ref, zpn_ref, ab_ref, f_ref, x_ref,
                    wdw_ref, bdw_ref, cg_ref, cb_ref, wpool_ref, bpool_ref, pscale_ref,
                    wf_ref, bf_ref, wout_ref, bout_ref, g_ref, beta_ref, o_ref, ob_ref,
                    uext_ref, conv_ref, pext_ref, psum_ref, zf_ref, mixed_ref, pf_ref, *, seq, alpha):
    ts = MIX_ROWS
    s = pl.program_id(1)
    first = s == 0
    last = s == pl.num_programs(1) - 1

    zprev = jnp.where(first, 0.0, zpp_ref[0])
    znext = jnp.where(last, 0.0, zpn_ref[0])
    t = s * ts + lax.broadcasted_iota(jnp.int32, (ts, 1), 0)
    for g, k in enumerate(POOL_WINDOWS):
        lanes = slice(g * HEAD_DIM, (g + 1) * HEAD_DIM)
        pext_ref[g, 0:POOL_HALO, :] = zprev[:, lanes]
        pext_ref[g, POOL_HALO:POOL_HALO + ts, :] = zp_ref[0, :, lanes]
        pext_ref[g, POOL_HALO + ts:2 * POOL_HALO + ts, :] = znext[:, lanes]
        pext_ref[g, 2 * POOL_HALO + ts:, :] = jnp.zeros(
            (PEXT_ROWS - 2 * POOL_HALO - ts, LANES), F32)
        memo = {}

        def window(width, c, g=g, memo=memo):
            key = (width, c)
            if key not in memo:
                if width == 1:
                    memo[key] = _strided(pext_ref, g, c)
                elif width == 2:
                    memo[key] = window(1, c - 1) + window(1, c)
                else:
                    q = width // 4
                    memo[key] = window(width // 2, c - q) + window(width // 2, c + q)
            return memo[key]

        for idx in range(STRIP):
            psum_ref.at[g][_strip_rows(idx), :] = window(k, POOL_HALO + idx)
        cnt = (jnp.minimum(t + k // 2, seq) - jnp.maximum(t - k // 2, 0)).astype(F32)
        pooled = psum_ref[g, 0:ts, :] / cnt - zp_ref[0, :, lanes]
        yp = _dot(pooled.astype(BF16), wpool_ref[0, g]) + bpool_ref[0, :, lanes]
        mixed_ref[:, CONV_DIM + g * HEAD_DIM:CONV_DIM + (g + 1) * HEAD_DIM] = (
            yp * pscale_ref[0, :, lanes]).astype(BF16)
        pool_anchor = yp[0:SUBLANES, :]

    uprev = jnp.where(first, 0.0, up_ref[0])
    unext = jnp.where(last, 0.0, un_ref[0])
    for l in range(CONV_SLABS):
        lanes = slice(l * LANES, (l + 1) * LANES)
        uext_ref[l, 0:CONV_HALO, :] = uprev[:, lanes]
        uext_ref[l, CONV_HALO:CONV_HALO + ts, :] = u_ref[0, :, lanes]
        uext_ref[l, CONV_HALO + ts:2 * CONV_HALO + ts, :] = unext[:, lanes]
        uext_ref[l, 2 * CONV_HALO + ts:, :] = jnp.zeros(
            (UEXT_ROWS - 2 * CONV_HALO - ts, LANES), F32)
    kp = seq // DFT_PIECES
    for l in range(CONV_SLABS):
        krows = slice((l % DFT_PIECES) * kp, (l % DFT_PIECES + 1) * kp)
        part = _dot(f_ref[l], ab_ref[l // DFT_PIECES, 0, krows, :])
        if l == 0:
            zf_ref[...] = part
        else:
            zf_ref[...] += part

        if l == CONV_SLABS - 1:
            for h in range(FOURIER_HEADS):
                hl = slice(h * HEAD_DIM, (h + 1) * HEAD_DIM)
                yf = _dot(zf_ref[:, hl].astype(BF16), wf_ref[0, h]) + bf_ref[0, :, hl]
                off = CONV_DIM + POOL_DIM + h * HEAD_DIM
                mixed_ref[:, off:off + HEAD_DIM] = yf.astype(BF16)
            part = yf
        bias = (jnp.broadcast_to(bdw_ref[0, l], (SUBLANES, LANES))
                + _zero_after(part[0:SUBLANES, 0:LANES]))
        if l == 1:
            bias = bias + _zero_after(pool_anchor)
        chained = l >= CONV_SLABS - CONV_CHAINED_SLABS
        i0 = 0
        for n in (CONV_GROUPS_CHAINED if chained else CONV_GROUPS):
            accs = [bias] * n
            for k_lo, k_hi in CONV_TAP_SPLITS:
                taps = {k: jnp.broadcast_to(wdw_ref[0, l, k:k + 1, :], (SUBLANES, LANES))
                        for k in range(k_lo, k_hi)}
                for i in range(i0 + 1 + k_lo, i0 + n + k_hi):
                    v = _strided(uext_ref, l, i)
                    for idx in range(max(i0, i - k_hi), min(i0 + n, i - k_lo)):
                        accs[idx - i0] = accs[idx - i0] + v * taps[i - idx - 1]
            for j, acc in enumerate(accs):
                conv_ref.at[l][_strip_rows(i0 + j), :] = acc
            i0 += n
            if chained:
                bias = bias + _zero_after(accs[-1])

    k1 = CONV_DIM
    pf = _dot(mixed_ref[:, k1:], wout_ref[0, k1:, :])
    pf_ref[...] = pf
    pf_anchor = _zero_after(pf[0:1, 0:1])

    for r0 in range(0, ts, LN_ROWS):
        rows = slice(r0, r0 + LN_ROWS)
        parts = [conv_ref[l, rows, :] for l in range(CONV_SLABS)]
        tot = parts[0]
        for v in parts[1:]:
            tot = tot + v
        mu = jnp.sum(tot, axis=-1, keepdims=True) * (1.0 / CONV_DIM)
        if r0 == ts - LN_ROWS:
            mu = mu + pf_anchor
        devs = [v - mu for v in parts]
        sq = devs[0] * devs[0]
        for d in devs[1:]:
            sq = sq + d * d
        rstd = lax.rsqrt(jnp.sum(sq, axis=-1, keepdims=True) * (1.0 / CONV_DIM) + LN_EPS)
        for l, d in enumerate(devs):
            lanes = slice(l * LANES, (l + 1) * LANES)
            y = d * rstd * cg_ref[0, :, lanes] + cb_ref[0, :, lanes]
            mixed_ref[rows, lanes] = (y * jax.nn.sigmoid(y)).astype(BF16)

    proj = pf_ref[...] + _dot(mixed_ref[:, :k1], wout_ref[0, :k1, :])
    y = alpha * x_ref[0] + proj + bout_ref[0]
    x1 = _layer_norm_rows(y, g_ref[0], beta_ref[0])
    o_ref[0] = x1
    ob_ref[0] = x1.astype(BF16)


def _mix_out(u, zp, ab, f, x, w_dw, b_dw, cg, cb, w_pool, b_pool, pscale, w_f, b_f,
             w_out, b_out, g, beta, layer, alpha):
    bsz, seq, _ = u.shape
    ts = MIX_ROWS
    n_s = seq // ts
    ch = ts // CONV_HALO
    ph = ts // POOL_HALO

    def lay3(shape, **kw):
        return pl.BlockSpec((1,) + shape, lambda b, s: (layer,) + (0,) * len(shape), **kw)

    kern = functools.partial(_mix_out_kernel, seq=seq, alpha=alpha)
    return pl.pallas_call(
        kern,
        grid=(bsz, n_s),
        in_specs=[
            pl.BlockSpec((1, ts, CONV_DIM), lambda b, s: (b, s, 0)),
            pl.BlockSpec((1, CONV_HALO, CONV_DIM),
                         lambda b, s: (b, jnp.maximum(s * ch - 1, 0), 0)),
            pl.BlockSpec((1, CONV_HALO, CONV_DIM),
                         lambda b, s: (b, jnp.minimum((s + 1) * ch, seq // CONV_HALO - 1), 0)),
            pl.BlockSpec((1, ts, POOL_DIM), lambda b, s: (b, s, 0)),
            pl.BlockSpec((1, POOL_HALO, POOL_DIM),
                         lambda b, s: (b, jnp.maximum(s * ph - 1, 0), 0)),
            pl.BlockSpec((1, POOL_HALO, POOL_DIM),
                         lambda b, s: (b, jnp.minimum((s + 1) * ph, seq // POOL_HALO - 1), 0)),
            pl.BlockSpec((2, 1, seq, FOURIER_DIM), lambda b, s: (0, b, 0, 0),
                         pipeline_mode=pl.Buffered(1)),
            pl.BlockSpec((2 * DFT_PIECES, ts, seq // DFT_PIECES), lambda b, s: (0, s, 0)),
            pl.BlockSpec((1, ts, D_MODEL), lambda b, s: (b, s, 0)),
            lay3((CONV_SLABS, CONV_WIDTH, LANES)),
            lay3((CONV_SLABS, 1, LANES)),
            lay3((1, CONV_DIM)),
            lay3((1, CONV_DIM)),
            lay3((POOL_GROUPS, HEAD_DIM, HEAD_DIM)),
            lay3((1, POOL_DIM)),
            lay3((1, POOL_DIM)),
            lay3((FOURIER_HEADS, HEAD_DIM, HEAD_DIM)),
            lay3((1, FOURIER_DIM)),
            lay3((MIX_DIM, D_MODEL), pipeline_mode=pl.Buffered(1)),
            lay3((1, D_MODEL)),
            lay3((1, D_MODEL)),
            lay3((1, D_MODEL)),
        ],
        out_specs=[pl.BlockSpec((1, ts, D_MODEL), lambda b, s: (b, s, 0)),
                   pl.BlockSpec((1, ts, D_MODEL), lambda b, s: (b, s, 0))],
        out_shape=[jax.ShapeDtypeStruct((bsz, seq, D_MODEL), F32),
                   jax.ShapeDtypeStruct((bsz, seq, D_MODEL), BF16)],
        scratch_shapes=[
            pltpu.VMEM((CONV_SLABS, UEXT_ROWS, LANES), F32),
            pltpu.VMEM((CONV_SLABS, STRIP_ROWS, LANES), F32),
            pltpu.VMEM((POOL_GROUPS, PEXT_ROWS, LANES), F32),
            pltpu.VMEM((POOL_GROUPS, STRIP_ROWS, LANES), F32),
            pltpu.VMEM((ts, FOURIER_DIM), F32),
            pltpu.VMEM((ts, MIX_DIM), BF16),
            pltpu.VMEM((ts, D_MODEL), F32),
        ],
        compiler_params=pltpu.CompilerParams(
            dimension_semantics=("arbitrary", "arbitrary"), vmem_limit_bytes=V7X_VMEM_LIMIT),
        name="mix_out",
    )(u, u, u, zp, zp, zp, ab, f, x, w_dw, b_dw, cg, cb, w_pool, b_pool, pscale, w_f, b_f,
      w_out, b_out, g, beta)


FF_CHUNK = 512
N_FF = D_FF // FF_CHUNK
FFN_ROWS = 1024
FFN_TAIL_ROWS = 256
N_TAIL = FFN_ROWS // FFN_TAIL_ROWS


def _ffn_kernel(xb_ref, x_ref, p_ref, wg_ref, wu_ref, wd_ref, wpg_ref, wple_ref, g_ref, beta_ref,
                o_ref, acc_ref, *, alpha):
    j = pl.program_id(1)

    def down_proj():
        xb = xb_ref[...]
        gate = _dot(xb, wg_ref[0])
        up = _dot(xb, wu_ref[0])
        hid = (gate * jax.nn.sigmoid(gate) * up).astype(BF16)
        return _dot(hid, wd_ref[0])

    @pl.when(j == 0)
    def _():
        acc_ref[...] = down_proj()

    @pl.when(jnp.logical_and(j > 0, j < N_FF))
    def _():
        acc_ref[...] += down_proj()

    @pl.when(j >= N_FF)
    def _():
        rows = pl.ds(pl.multiple_of((j - N_FF) * FFN_TAIL_ROWS, FFN_TAIL_ROWS), FFN_TAIL_ROWS)
        gate = jax.nn.sigmoid(_dot(xb_ref[rows, :], wpg_ref[0]))
        emb = _dot(p_ref[0].astype(BF16), wple_ref[0])
        y = alpha * x_ref[...] + acc_ref[rows, :] + gate * emb
        o_ref[...] = _layer_norm_rows(y, g_ref[0], beta_ref[0])


def _ffn(xb, x, p, w_gate, w_up, w_down, w_pg, w_ple, g, beta, layer, alpha):
    m = x.shape[0]
    last_ff = N_FF - 1

    def ff_chunk(i, j):
        return (layer, 0, jnp.minimum(j, last_ff))

    def tail_rows(i, j):
        return i * N_TAIL + jnp.maximum(j - N_FF, 0)

    def lay3(shape, **kw):
        return pl.BlockSpec((1,) + shape, lambda i, j: (layer,) + (0,) * len(shape), **kw)

    return pl.pallas_call(
        functools.partial(_ffn_kernel, alpha=alpha),
        grid=(m // FFN_ROWS, N_FF + N_TAIL),
        in_specs=[
            pl.BlockSpec((FFN_ROWS, D_MODEL), lambda i, j: (i, 0)),
            pl.BlockSpec((FFN_TAIL_ROWS, D_MODEL), lambda i, j: (tail_rows(i, j), 0)),
            pl.BlockSpec((1, FFN_TAIL_ROWS, PLE_DIM), lambda i, j: (layer, tail_rows(i, j), 0)),
            pl.BlockSpec((1, D_MODEL, FF_CHUNK), ff_chunk),
            pl.BlockSpec((1, D_MODEL, FF_CHUNK), ff_chunk),
            pl.BlockSpec((1, FF_CHUNK, D_MODEL), lambda i, j: (layer, jnp.minimum(j, last_ff), 0)),
            lay3((D_MODEL, D_MODEL), pipeline_mode=pl.Buffered(1)),
            lay3((PLE_DIM, D_MODEL), pipeline_mode=pl.Buffered(1)),
            lay3((1, D_MODEL)),
            lay3((1, D_MODEL)),
        ],
        out_specs=pl.BlockSpec((FFN_TAIL_ROWS, D_MODEL), lambda i, j: (tail_rows(i, j), 0)),
        out_shape=jax.ShapeDtypeStruct((m, D_MODEL), F32),
        scratch_shapes=[pltpu.VMEM((FFN_ROWS, D_MODEL), F32)],
        compiler_params=pltpu.CompilerParams(
            dimension_semantics=("arbitrary", "arbitrary"), vmem_limit_bytes=V7X_VMEM_LIMIT),
        name="ffn",
    )(xb, x, p, w_gate, w_up, w_down, w_pg, w_ple, g, beta)


def _dft_tables(seq):
    c = jnp.arange(HEAD_DIM, dtype=jnp.int32)
    ang = ((c[:, None] * c[None, :]) % HEAD_DIM).astype(F32) * (2.0 * math.pi / HEAD_DIM)
    cs = jnp.concatenate([jnp.cos(ang), jnp.sin(ang)], axis=1) * (HEAD_DIM ** -0.5)
    r = 64
    k = jnp.arange(seq, dtype=jnp.int32)
    j1 = jnp.arange(seq // r, dtype=jnp.int32)
    j0 = jnp.arange(r, dtype=jnp.int32)
    a1 = ((r * j1[:, None] * k[None, :]) % seq).astype(F32) * (2.0 * math.pi / seq)
    a0 = ((j0[:, None] * k[None, :]) % seq).astype(F32) * (2.0 * math.pi / seq)
    kp = seq // DFT_PIECES

    def pieces(small):
        return small.reshape(small.shape[0], DFT_PIECES, kp).transpose(1, 0, 2)

    c1, s1 = pieces(jnp.cos(a1))[:, :, None, :], pieces(jnp.sin(a1))[:, :, None, :]
    c0, s0 = pieces(jnp.cos(a0))[:, None, :, :], pieces(jnp.sin(a0))[:, None, :, :]
    scale = seq ** -0.5
    fc = ((c1 * c0 - s1 * s0) * scale).reshape(DFT_PIECES, seq, kp)
    fs = ((s1 * c0 + c1 * s0) * (-scale)).reshape(DFT_PIECES, seq, kp)
    f = jnp.concatenate([fc.astype(BF16), fs.astype(BF16)], axis=0)
    return cs.astype(BF16), f


def kernel(x, p, w_in, b_in, w_dw, b_dw, conv_ln_g, conv_ln_b, w_pool, b_pool, pool_scale,
           w_fourier, b_fourier, w_out, b_out, ln1_g, ln1_b, w_gate, w_up, w_down, w_ple,
           w_ple_gate, ln2_g, ln2_b):
    bsz, seq, d = x.shape
    depth = w_in.shape[0]
    m = bsz * seq
    alpha = (2 * depth) ** 0.25
    tm = IN_ROWS
    assert seq % MIX_ROWS == 0 and seq % tm == 0 and m % FFN_ROWS == 0

    cs, f_tbl = _dft_tables(seq)
    w_in_b = w_in.astype(BF16)
    w_pool_b = w_pool.astype(BF16)
    w_f_b = w_fourier.astype(BF16)
    w_out_b = w_out.astype(BF16)
    w_gate_b = w_gate.astype(BF16)
    w_up_b = w_up.astype(BF16)
    w_down_b = w_down.astype(BF16)
    w_ple_b = w_ple.astype(BF16)
    w_pg_b = w_ple_gate.astype(BF16)

    def row3(a):
        return a.reshape(a.shape[0], 1, -1)

    b_in3, cg3, cb3 = row3(b_in), row3(conv_ln_g), row3(conv_ln_b)
    w_dw4 = w_dw.reshape(depth, CONV_WIDTH, CONV_SLABS, LANES).transpose(0, 2, 1, 3)
    b_dw4 = b_dw.reshape(depth, CONV_SLABS, 1, LANES)
    b_pool3, pscale3, b_f3 = row3(b_pool), row3(pool_scale), row3(b_fourier)
    b_out3, g1, be1, g2, be2 = row3(b_out), row3(ln1_g), row3(ln1_b), row3(ln2_g), row3(ln2_b)
    p2 = p.reshape(depth, m, PLE_DIM)

    xf = x.reshape(m, d)
    for l in range(depth):
        u, zp, ab = _in_proj(xf, w_in_b, b_in3, cs, l, tm)
        x1, x1b = _mix_out(u.reshape(bsz, seq, CONV_DIM), zp.reshape(bsz, seq, POOL_DIM),
                           ab.reshape(2, bsz, seq, FOURIER_DIM), f_tbl, xf.reshape(bsz, seq, d),
                           w_dw4, b_dw4, cg3, cb3, w_pool_b, b_pool3, pscale3, w_f_b, b_f3,
                           w_out_b, b_out3, g1, be1, l, alpha)
        xf = _ffn(x1b.reshape(m, d), x1.reshape(m, d), p2, w_gate_b, w_up_b, w_down_b, w_pg_b,
                  w_ple_b, g2, be2, l, alpha)
    return xf.reshape(bsz, seq, d)
```

```python
import functools
import math

import jax
import jax.numpy as jnp
from jax import lax
from jax.experimental import pallas as pl
from jax.experimental.pallas import tpu as pltpu

D_MODEL = 2048
PLE_DIM = 256
CONV_WIDTH = 31
HEAD_DIM = 128
CONV_DIM = 1024
POOL_GROUPS = 4
POOL_DIM = 512
FOURIER_HEADS = 4
FOURIER_DIM = 512
MIX_DIM = 2048
IN_DIM = 3072
POOL_WINDOWS = (2, 4, 8, 16)
D_FF = 5632
LN_EPS = 1e-5

LANES = 128
SUBLANES = 8
CONV_HALO = 16
POOL_HALO = 8
V7X_VMEM_LIMIT = 56 * 1024 * 1024

BF16 = jnp.bfloat16
F32 = jnp.float32


def _dot(a, b):
    return jnp.dot(a, b, preferred_element_type=F32)


def _layer_norm_rows(v, g, b):
    mu = jnp.mean(v, axis=-1, keepdims=True)
    d = v - mu
    var = jnp.mean(d * d, axis=-1, keepdims=True)
    return d * lax.rsqrt(var + LN_EPS) * g + b


def _const_spec(shape):
    nd = len(shape)
    return pl.BlockSpec(shape, lambda *_: (0,) * nd)


IN_CHUNK = 512
IN_ROWS = 1024


def _in_proj_kernel(x_ref, w_ref, b_ref, cs_ref, u_ref, zp_ref, ab_ref):
    xb = x_ref[...].astype(BF16)

    def z_cols(lo):
        return _dot(xb, w_ref[0, :, lo:lo + IN_CHUNK]) + b_ref[0, :, lo:lo + IN_CHUNK]

    for c in range(CONV_DIM // IN_CHUNK):
        val = z_cols(c * IN_CHUNK)
        gate = z_cols(CONV_DIM + c * IN_CHUNK)
        u_ref[:, c * IN_CHUNK:(c + 1) * IN_CHUNK] = val * jax.nn.sigmoid(gate)
    zp_ref[...] = z_cols(2 * CONV_DIM)
    zf = z_cols(2 * CONV_DIM + POOL_DIM).astype(BF16)
    for h in range(FOURIER_HEADS):
        ab = _dot(zf[:, h * HEAD_DIM:(h + 1) * HEAD_DIM], cs_ref[...])
        ab_ref[0, :, h * HEAD_DIM:(h + 1) * HEAD_DIM] = ab[:, :HEAD_DIM].astype(BF16)
        ab_ref[1, :, h * HEAD_DIM:(h + 1) * HEAD_DIM] = ab[:, HEAD_DIM:].astype(BF16)


def _in_proj(x, w_in, b_in, cs, layer, tm):
    m = x.shape[0]
    return pl.pallas_call(
        _in_proj_kernel,
        grid=(m // tm,),
        in_specs=[
            pl.BlockSpec((tm, D_MODEL), lambda i: (i, 0)),
            pl.BlockSpec((1, D_MODEL, IN_DIM), lambda i: (layer, 0, 0),
                         pipeline_mode=pl.Buffered(1)),
            pl.BlockSpec((1, 1, IN_DIM), lambda i: (layer, 0, 0)),
            _const_spec((HEAD_DIM, 2 * HEAD_DIM)),
        ],
        out_specs=[
            pl.BlockSpec((tm, CONV_DIM), lambda i: (i, 0)),
            pl.BlockSpec((tm, POOL_DIM), lambda i: (i, 0)),
            pl.BlockSpec((2, tm, FOURIER_DIM), lambda i: (0, i, 0)),
        ],
        out_shape=[
            jax.ShapeDtypeStruct((m, CONV_DIM), F32),
            jax.ShapeDtypeStruct((m, POOL_DIM), F32),
            jax.ShapeDtypeStruct((2, m, FOURIER_DIM), BF16),
        ],
        compiler_params=pltpu.CompilerParams(
            dimension_semantics=("arbitrary",), vmem_limit_bytes=V7X_VMEM_LIMIT),
        name="in_proj",
    )(x, w_in, b_in, cs)


MIX_ROWS = 256
STRIP = 34
STRIP_ROWS = STRIP * SUBLANES
CONV_SLABS = CONV_DIM // LANES
UEXT_ROWS = 304
PEXT_ROWS = 288
CONV_GROUPS = (9, 9, 8, 8)
CONV_CHAINED_SLABS = 2
CONV_GROUPS_CHAINED = (3,) * 11 + (1,)
CONV_TAP_SPLITS = ((0, 16), (16, CONV_WIDTH))
DFT_PIECES = CONV_SLABS // 2
assert sum(CONV_GROUPS) == STRIP and sum(CONV_GROUPS_CHAINED) == STRIP
LN_ROWS = 64


def _strip_rows(i):
    return pl.ds(i, SUBLANES, stride=STRIP)


def _strided(ref, slab, i):
    return ref.at[slab][_strip_rows(i), :]


def _zero_after(v):
    bits = lax.bitcast_convert_type(v, jnp.uint32)
    half = jnp.uint32(16)
    zero = lax.shift_right_logical(lax.shift_right_logical(bits, half), half)
    return lax.bitcast_convert_type(zero, F32)


def _mix_out_kernel(u_ref, up_ref, un_ref, zp_ref, zpp_ref, zpn_ref, ab_ref, f_ref, x_ref,
                    wdw_ref, bdw_ref, cg_ref, cb_ref, wpool_ref, bpool_ref, pscale_ref,
                    wf_ref, bf_ref, wout_ref, bout_ref, g_ref, beta_ref, o_ref, ob_ref,
                    uext_ref, conv_ref, pext_ref, psum_ref, zf_ref, mixed_ref, pf_ref, *, seq, alpha):
    ts = MIX_ROWS
    s = pl.program_id(1)
    first = s == 0
    last = s == pl.num_programs(1) - 1

    zprev = jnp.where(first, 0.0, zpp_ref[0])
    znext = jnp.where(last, 0.0, zpn_ref[0])
    t = s * ts + lax.broadcasted_iota(jnp.int32, (ts, 1), 0)
    for g, k in enumerate(POOL_WINDOWS):
        lanes = slice(g * HEAD_DIM, (g + 1) * HEAD_DIM)
        pext_ref[g, 0:POOL_HALO, :] = zprev[:, lanes]
        pext_ref[g, POOL_HALO:POOL_HALO + ts, :] = zp_ref[0, :, lanes]
        pext_ref[g, POOL_HALO + ts:2 * POOL_HALO + ts, :] = znext[:, lanes]
        pext_ref[g, 2 * POOL_HALO + ts:, :] = jnp.zeros(
            (PEXT_ROWS - 2 * POOL_HALO - ts, LANES), F32)
        memo = {}

        def window(width, c, g=g, memo=memo):
            key = (width, c)
            if key not in memo:
                if width == 1:
                    memo[key] = _strided(pext_ref, g, c)
                elif width == 2:
                    memo[key] = window(1, c - 1) + window(1, c)
                else:
                    q = width // 4
                    memo[key] = window(width // 2, c - q) + window(width // 2, c + q)
            return memo[key]

        for idx in range(STRIP):
            psum_ref.at[g][_strip_rows(idx), :] = window(k, POOL_HALO + idx)
        cnt = (jnp.minimum(t + k // 2, seq) - jnp.maximum(t - k // 2, 0)).astype(F32)
        pooled = psum_ref[g, 0:ts, :] / cnt - zp_ref[0, :, lanes]
        yp = _dot(pooled.astype(BF16), wpool_ref[0, g]) + bpool_ref[0, :, lanes]
        mixed_ref[:, CONV_DIM + g * HEAD_DIM:CONV_DIM + (g + 1) * HEAD_DIM] = (
            yp * pscale_ref[0, :, lanes]).astype(BF16)
        pool_anchor = yp[0:SUBLANES, :]

    uprev = jnp.where(first, 0.0, up_ref[0])
    unext = jnp.where(last, 0.0, un_ref[0])
    for l in range(CONV_SLABS):
        lanes = slice(l * LANES, (l + 1) * LANES)
        uext_ref[l, 0:CONV_HALO, :] = uprev[:, lanes]
        uext_ref[l, CONV_HALO:CONV_HALO + ts, :] = u_ref[0, :, lanes]
        uext_ref[l, CONV_HALO + ts:2 * CONV_HALO + ts, :] = unext[:, lanes]
        uext_ref[l, 2 * CONV_HALO + ts:, :] = jnp.zeros(
            (UEXT_ROWS - 2 * CONV_HALO - ts, LANES), F32)
    kp = seq // DFT_PIECES
    for l in range(CONV_SLABS):
        krows = slice((l % DFT_PIECES) * kp, (l % DFT_PIECES + 1) * kp)
        part = _dot(f_ref[l], ab_ref[l // DFT_PIECES, 0, krows, :])
        if l == 0:
            zf_ref[...] = part
        else:
            zf_ref[...] += part

        if l == CONV_SLABS - 1:
            for h in range(FOURIER_HEADS):
                hl = slice(h * HEAD_DIM, (h + 1) * HEAD_DIM)
                yf = _dot(zf_ref[:, hl].astype(BF16), wf_ref[0, h]) + bf_ref[0, :, hl]
                off = CONV_DIM + POOL_DIM + h * HEAD_DIM
                mixed_ref[:, off:off + HEAD_DIM] = yf.astype(BF16)
            part = yf
        bias = (jnp.broadcast_to(bdw_ref[0, l], (SUBLANES, LANES))
                + _zero_after(part[0:SUBLANES, 0:LANES]))
        if l == 1:
            bias = bias + _zero_after(pool_anchor)
        chained = l >= CONV_SLABS - CONV_CHAINED_SLABS
        i0 = 0
        for n in (CONV_GROUPS_CHAINED if chained else CONV_GROUPS):
            accs = [bias] * n
            for k_lo, k_hi in CONV_TAP_SPLITS:
                taps = {k: jnp.broadcast_to(wdw_ref[0, l, k:k + 1, :], (SUBLANES, LANES))
                        for k in range(k_lo, k_hi)}
                for i in range(i0 + 1 + k_lo, i0 + n + k_hi):
                    v = _strided(uext_ref, l, i)
                    for idx in range(max(i0, i - k_hi), min(i0 + n, i - k_lo)):
                        accs[idx - i0] = accs[idx - i0] + v * taps[i - idx - 1]
            for j, acc in enumerate(accs):
                conv_ref.at[l][_strip_rows(i0 + j), :] = acc
            i0 += n
            if chained:
                bias = bias + _zero_after(accs[-1])

    k1 = CONV_DIM
    pf = _dot(mixed_ref[:, k1:], wout_ref[0, k1:, :])
    pf_ref[...] = pf
    pf_anchor = _zero_after(pf[0:1, 0:1])

    for r0 in range(0, ts, LN_ROWS):
        rows = slice(r0, r0 + LN_ROWS)
        parts = [conv_ref[l, rows, :] for l in range(CONV_SLABS)]
        tot = parts[0]
        for v in parts[1:]:
            tot = tot + v
        mu = jnp.sum(tot, axis=-1, keepdims=True) * (1.0 / CONV_DIM)
        if r0 == ts - LN_ROWS:
            mu = mu + pf_anchor
        devs = [v - mu for v in parts]
        sq = devs[0] * devs[0]
        for d in devs[1:]:
            sq = sq + d * d
        rstd = lax.rsqrt(jnp.sum(sq, axis=-1, keepdims=True) * (1.0 / CONV_DIM) + LN_EPS)
        for l, d in enumerate(devs):
            lanes = slice(l * LANES, (l + 1) * LANES)
            y = d * rstd * cg_ref[0, :, lanes] + cb_ref[0, :, lanes]
            mixed_ref[rows, lanes] = (y * jax.nn.sigmoid(y)).astype(BF16)

    proj = pf_ref[...] + _dot(mixed_ref[:, :k1], wout_ref[0, :k1, :])
    y = alpha * x_ref[0] + proj + bout_ref[0]
    x1 = _layer_norm_rows(y, g_ref[0], beta_ref[0])
    o_ref[0] = x1
    ob_ref[0] = x1.astype(BF16)


def _mix_out(u, zp, ab, f, x, w_dw, b_dw, cg, cb, w_pool, b_pool, pscale, w_f, b_f,
             w_out, b_out, g, beta, layer, alpha):
    bsz, seq, _ = u.shape
    ts = MIX_ROWS
    n_s = seq // ts
    ch = ts // CONV_HALO
    ph = ts // POOL_HALO

    def lay3(shape, **kw):
        return pl.BlockSpec((1,) + shape, lambda b, s: (layer,) + (0,) * len(shape), **kw)

    kern = functools.partial(_mix_out_kernel, seq=seq, alpha=alpha)
    return pl.pallas_call(
        kern,
        grid=(bsz, n_s),
        in_specs=[
            pl.BlockSpec((1, ts, CONV_DIM), lambda b, s: (b, s, 0)),
            pl.BlockSpec((1, CONV_HALO, CONV_DIM),
                         lambda b, s: (b, jnp.maximum(s * ch - 1, 0), 0)),
            pl.BlockSpec((1, CONV_HALO, CONV_DIM),
                         lambda b, s: (b, jnp.minimum((s + 1) * ch, seq // CONV_HALO - 1), 0)),
            pl.BlockSpec((1, ts, POOL_DIM), lambda b, s: (b, s, 0)),
            pl.BlockSpec((1, POOL_HALO, POOL_DIM),
                         lambda b, s: (b, jnp.maximum(s * ph - 1, 0), 0)),
            pl.BlockSpec((1, POOL_HALO, POOL_DIM),
                         lambda b, s: (b, jnp.minimum((s + 1) * ph, seq // POOL_HALO - 1), 0)),
            pl.BlockSpec((2, 1, seq, FOURIER_DIM), lambda b, s: (0, b, 0, 0)),
            pl.BlockSpec((2 * DFT_PIECES, ts, seq // DFT_PIECES), lambda b, s: (0, s, 0)),
            pl.BlockSpec((1, ts, D_MODEL), lambda b, s: (b, s, 0)),
            lay3((CONV_SLABS, CONV_WIDTH, LANES)),
            lay3((CONV_SLABS, 1, LANES)),
            lay3((1, CONV_DIM)),
            lay3((1, CONV_DIM)),
            lay3((POOL_GROUPS, HEAD_DIM, HEAD_DIM)),
            lay3((1, POOL_DIM)),
            lay3((1, POOL_DIM)),
            lay3((FOURIER_HEADS, HEAD_DIM, HEAD_DIM)),
            lay3((1, FOURIER_DIM)),
            lay3((MIX_DIM, D_MODEL), pipeline_mode=pl.Buffered(1)),
            lay3((1, D_MODEL)),
            lay3((1, D_MODEL)),
            lay3((1, D_MODEL)),
        ],
        out_specs=[pl.BlockSpec((1, ts, D_MODEL), lambda b, s: (b, s, 0)),
                   pl.BlockSpec((1, ts, D_MODEL), lambda b, s: (b, s, 0))],
        out_shape=[jax.ShapeDtypeStruct((bsz, seq, D_MODEL), F32),
                   jax.ShapeDtypeStruct((bsz, seq, D_MODEL), BF16)],
        scratch_shapes=[
            pltpu.VMEM((CONV_SLABS, UEXT_ROWS, LANES), F32),
            pltpu.VMEM((CONV_SLABS, STRIP_ROWS, LANES), F32),
            pltpu.VMEM((POOL_GROUPS, PEXT_ROWS, LANES), F32),
            pltpu.VMEM((POOL_GROUPS, STRIP_ROWS, LANES), F32),
            pltpu.VMEM((ts, FOURIER_DIM), F32),
            pltpu.VMEM((ts, MIX_DIM), BF16),
            pltpu.VMEM((ts, D_MODEL), F32),
        ],
        compiler_params=pltpu.CompilerParams(
            dimension_semantics=("arbitrary", "arbitrary"), vmem_limit_bytes=V7X_VMEM_LIMIT),
        name="mix_out",
    )(u, u, u, zp, zp, zp, ab, f, x, w_dw, b_dw, cg, cb, w_pool, b_pool, pscale, w_f, b_f,
      w_out, b_out, g, beta)


FF_CHUNK = 512
N_FF = D_FF // FF_CHUNK
FFN_ROWS = 1024
FFN_TAIL_ROWS = 256
N_TAIL = FFN_ROWS // FFN_TAIL_ROWS


def _ffn_kernel(xb_ref, x_ref, p_ref, wg_ref, wu_ref, wd_ref, wpg_ref, wple_ref, g_ref, beta_ref,
                o_ref, acc_ref, *, alpha):
    j = pl.program_id(1)

    def down_proj():
        xb = xb_ref[...]
        gate = _dot(xb, wg_ref[0])
        up = _dot(xb, wu_ref[0])
        hid = (gate * jax.nn.sigmoid(gate) * up).astype(BF16)
        return _dot(hid, wd_ref[0])

    @pl.when(j == 0)
    def _():
        acc_ref[...] = down_proj()

    @pl.when(jnp.logical_and(j > 0, j < N_FF))
    def _():
        acc_ref[...] += down_proj()

    @pl.when(j >= N_FF)
    def _():
        rows = pl.ds(pl.multiple_of((j - N_FF) * FFN_TAIL_ROWS, FFN_TAIL_ROWS), FFN_TAIL_ROWS)
        gate = jax.nn.sigmoid(_dot(xb_ref[rows, :], wpg_ref[0]))
        emb = _dot(p_ref[0].astype(BF16), wple_ref[0])
        y = alpha * x_ref[...] + acc_ref[rows, :] + gate * emb
        o_ref[...] = _layer_norm_rows(y, g_ref[0], beta_ref[0])


def _ffn(xb, x, p, w_gate, w_up, w_down, w_pg, w_ple, g, beta, layer, alpha):
    m = x.shape[0]
    last_ff = N_FF - 1

    def ff_chunk(i, j):
        return (layer, 0, jnp.minimum(j, last_ff))

    def tail_rows(i, j):
        return i * N_TAIL + jnp.maximum(j - N_FF, 0)

    def lay3(shape, **kw):
        return pl.BlockSpec((1,) + shape, lambda i, j: (layer,) + (0,) * len(shape), **kw)

    return pl.pallas_call(
        functools.partial(_ffn_kernel, alpha=alpha),
        grid=(m // FFN_ROWS, N_FF + N_TAIL),
        in_specs=[
            pl.BlockSpec((FFN_ROWS, D_MODEL), lambda i, j: (i, 0)),
            pl.BlockSpec((FFN_TAIL_ROWS, D_MODEL), lambda i, j: (tail_rows(i, j), 0)),
            pl.BlockSpec((1, FFN_TAIL_ROWS, PLE_DIM), lambda i, j: (layer, tail_rows(i, j), 0)),
            pl.BlockSpec((1, D_MODEL, FF_CHUNK), ff_chunk),
            pl.BlockSpec((1, D_MODEL, FF_CHUNK), ff_chunk),
            pl.BlockSpec((1, FF_CHUNK, D_MODEL), lambda i, j: (layer, jnp.minimum(j, last_ff), 0)),
            lay3((D_MODEL, D_MODEL), pipeline_mode=pl.Buffered(1)),
            lay3((PLE_DIM, D_MODEL), pipeline_mode=pl.Buffered(1)),
            lay3((1, D_MODEL)),
            lay3((1, D_MODEL)),
        ],
        out_specs=pl.BlockSpec((FFN_TAIL_ROWS, D_MODEL), lambda i, j: (tail_rows(i, j), 0)),
        out_shape=jax.ShapeDtypeStruct((m, D_MODEL), F32),
        scratch_shapes=[pltpu.VMEM((FFN_ROWS, D_MODEL), F32)],
        compiler_params=pltpu.CompilerParams(
            dimension_semantics=("arbitrary", "arbitrary"), vmem_limit_bytes=V7X_VMEM_LIMIT),
        name="ffn",
    )(xb, x, p, w_gate, w_up, w_down, w_pg, w_ple, g, beta)


def _dft_tables(seq):
    c = jnp.arange(HEAD_DIM, dtype=jnp.int32)
    ang = ((c[:, None] * c[None, :]) % HEAD_DIM).astype(F32) * (2.0 * math.pi / HEAD_DIM)
    cs = jnp.concatenate([jnp.cos(ang), jnp.sin(ang)], axis=1) * (HEAD_DIM ** -0.5)
    r = 64
    k = jnp.arange(seq, dtype=jnp.int32)
    j1 = jnp.arange(seq // r, dtype=jnp.int32)
    j0 = jnp.arange(r, dtype=jnp.int32)
    a1 = ((r * j1[:, None] * k[None, :]) % seq).astype(F32) * (2.0 * math.pi / seq)
    a0 = ((j0[:, None] * k[None, :]) % seq).astype(F32) * (2.0 * math.pi / seq)
    kp = seq // DFT_PIECES

    def pieces(small):
        return small.reshape(small.shape[0], DFT_PIECES, kp).transpose(1, 0, 2)

    c1, s1 = pieces(jnp.cos(a1))[:, :, None, :], pieces(jnp.sin(a1))[:, :, None, :]
    c0, s0 = pieces(jnp.cos(a0))[:, None, :, :], pieces(jnp.sin(a0))[:, None, :, :]
    scale = seq ** -0.5
    fc = ((c1 * c0 - s1 * s0) * scale).reshape(DFT_PIECES, seq, kp)
    fs = ((s1 * c0 + c1 * s0) * (-scale)).reshape(DFT_PIECES, seq, kp)
    f = jnp.concatenate([fc.astype(BF16), fs.astype(BF16)], axis=0)
    return cs.astype(BF16), f


def kernel(x, p, w_in, b_in, w_dw, b_dw, conv_ln_g, conv_ln_b, w_pool, b_pool, pool_scale,
           w_fourier, b_fourier, w_out, b_out, ln1_g, ln1_b, w_gate, w_up, w_down, w_ple,
           w_ple_gate, ln2_g, ln2_b):
    bsz, seq, d = x.shape
    depth = w_in.shape[0]
    m = bsz * seq
    alpha = (2 * depth) ** 0.25
    tm = IN_ROWS
    assert seq % MIX_ROWS == 0 and seq % tm == 0 and m % FFN_ROWS == 0

    cs, f_tbl = _dft_tables(seq)
    w_in_b = w_in.astype(BF16)
    w_pool_b = w_pool.astype(BF16)
    w_f_b = w_fourier.astype(BF16)
    w_out_b = w_out.astype(BF16)
    w_gate_b = w_gate.astype(BF16)
    w_up_b = w_up.astype(BF16)
    w_down_b = w_down.astype(BF16)
    w_ple_b = w_ple.astype(BF16)
    w_pg_b = w_ple_gate.astype(BF16)

    def row3(a):
        return a.reshape(a.shape[0], 1, -1)

    b_in3, cg3, cb3 = row3(b_in), row3(conv_ln_g), row3(conv_ln_b)
    w_dw4 = w_dw.reshape(depth, CONV_WIDTH, CONV_SLABS, LANES).transpose(0, 2, 1, 3)
    b_dw4 = b_dw.reshape(depth, CONV_SLABS, 1, LANES)
    b_pool3, pscale3, b_f3 = row3(b_pool), row3(pool_scale), row3(b_fourier)
    b_out3, g1, be1, g2, be2 = row3(b_out), row3(ln1_g), row3(ln1_b), row3(ln2_g), row3(ln2_b)
    p2 = p.reshape(depth, m, PLE_DIM)

    xf = x.reshape(m, d)
    for l in range(depth):
        u, zp, ab = _in_proj(xf, w_in_b, b_in3, cs, l, tm)
        x1, x1b = _mix_out(u.reshape(bsz, seq, CONV_DIM), zp.reshape(bsz, seq, POOL_DIM),
                           ab.reshape(2, bsz, seq, FOURIER_DIM), f_tbl, xf.reshape(bsz, seq, d),
                           w_dw4, b_dw4, cg3, cb3, w_pool_b, b_pool3, pscale3, w_f_b, b_f3,
                           w_out_b, b_out3, g1, be1, l, alpha)
        xf = _ffn(x1b.reshape(m, d), x1.reshape(m, d), p2, w_gate_b, w_up_b, w_down_b, w_pg_b,
                  w_ple_b, g2, be2, l, alpha)
    return xf.reshape(bsz, seq, d)
```
